```python
import math, functools
import jax, jax.numpy as jnp
from jax import lax
import numpy as np

D_MODEL = 1024
BATCH = 2
SEQ = 8192
DEPTH = 2
DEC_BATCH = 32
DEC_SEQ = 1
PAST_LEN = 16384
PAGE_SIZE = 128

HG_WIDTH = D_MODEL // 2
HG_HEAD_DIM = 128
HG_HEADS = HG_WIDTH // HG_HEAD_DIM
HG_CHUNK = 16
HEAD_DIM = 64
N_HEADS = (D_MODEL // 2) // HEAD_DIM
KV_HEADS = N_HEADS
ATT_WIDTH = N_HEADS * HEAD_DIM
MOBA_BLOCK = 256
MOBA_TOPK = 3
Q_CHUNK = 64
D_FF = ((8 * D_MODEL // 3 + 127) // 128) * 128
N_EXPERTS = 8
TOP_K = 2
MOE_MAX_ROWS = 256
N_DENSE = (DEPTH + 1) // 2
N_MOE = DEPTH // 2
RMS_EPS = 1e-6
IN_SPLITS = (HG_WIDTH,) * 4 + (ATT_WIDTH,) * 3 + (D_MODEL,) * 2
IN_WIDTH = sum(IN_SPLITS)

kernel_name = 'hgrn2_moba_gated_hybrid_step'

F32 = jnp.float32


def rmsnorm(x, g):
    xf = x.astype(F32)
    y = xf * lax.rsqrt(jnp.mean(xf * xf, axis=-1, keepdims=True) + RMS_EPS)
    return (y * g.astype(F32)).astype(x.dtype)


def alibi_slopes():
    return 2.0 ** (-8.0 * jnp.arange(1, N_HEADS + 1, dtype=F32) / N_HEADS)


def split_in(z):
    idx = np.cumsum(IN_SPLITS)[:-1].tolist()
    return jnp.split(z, idx, axis=-1)


def gla_chunked(q, k, v, logf, s0):
    B, T, H, K = q.shape
    C = math.gcd(T, HG_CHUNK)
    N = T // C

    def blk(a):
        return a.reshape(B, N, C, H, a.shape[-1]).transpose(1, 0, 3, 2, 4)

    q, k, v, logf = blk(q), blk(k), blk(v), blk(logf)
    b = jnp.cumsum(logf, axis=3)
    b_last = b[:, :, :, -1:]
    q_in = q * jnp.exp(b)
    k_in = k * jnp.exp(-b)
    k_out = k * jnp.exp(b_last - b)
    causal = jnp.tril(jnp.ones((C, C), bool))
    att = jnp.where(causal, jnp.einsum('nbhtk,nbhsk->nbhts', q_in, k_in), 0.0)
    o_intra = jnp.einsum('nbhts,nbhsv->nbhtv', att, v)

    def step(S, inp):
        qi, ko, vc, dec = inp
        o_prev = jnp.einsum('bhtk,bhkv->bhtv', qi, S)
        S = S * dec[..., None] + jnp.einsum('bhsk,bhsv->bhkv', ko, vc)
        return S, o_prev

    S, o_inter = lax.scan(step, s0, (q_in, k_out, v, jnp.exp(b_last[:, :, :, 0])))
    o = (o_intra + o_inter).transpose(1, 0, 3, 2, 4).reshape(B, T, H, v.shape[-1])
    return o, S


def hgrn2_mixer(zq, zf, zi, zg, lb, g_out, s0):
    B, T, _ = zq.shape
    heads = lambda a: a.astype(F32).reshape(B, T, HG_HEADS, HG_HEAD_DIM)
    zf = zf.astype(F32)
    log_f = jnp.logaddexp(jnp.log(lb), jnp.log1p(-lb) + jax.nn.log_sigmoid(zf))
    k = (1.0 - lb) * jax.nn.sigmoid(-zf)
    o, s_new = gla_chunked(heads(zq), heads(k), heads(zi), heads(log_f), s0.astype(F32))
    o = rmsnorm(o, g_out) * jax.nn.silu(heads(zg))
    return o.reshape(B, T, HG_WIDTH).astype(zq.dtype), s_new.astype(s0.dtype)


def moba_core(q, k_loc, v_loc, pos_loc, qpos, slopes, sel=None):
    scale = HEAD_DIM ** -0.5
    dist_loc = (qpos[:, None] - pos_loc[None, :]).astype(F32)
    s_loc = jnp.einsum('bqhd,blhd->bqhl', q, k_loc).astype(F32) * scale - slopes[:, None] * dist_loc[:, None, :]
    s_loc = jnp.where((dist_loc >= 0)[:, None, :], s_loc, -jnp.inf)
    if sel is None:
        p = jax.nn.softmax(s_loc, axis=-1).astype(v_loc.dtype)
        return jnp.einsum('bqhl,blhd->bqhd', p, v_loc)
    k_sel, v_sel, pos_sel, ok = sel
    dist_sel = (qpos[None, :, None, None] - pos_sel).astype(F32)
    s_sel = jnp.einsum('bqhd,bqhmd->bqhm', q, k_sel).astype(F32) * scale - slopes[:, None] * dist_sel
    s_sel = jnp.where(ok, s_sel, -jnp.inf)
    m = s_sel.shape[-1]
    p = jax.nn.softmax(jnp.concatenate([s_sel, s_loc], axis=-1), axis=-1).astype(v_loc.dtype)
    return (jnp.einsum('bqhm,bqhmd->bqhd', p[..., :m], v_sel)
            + jnp.einsum('bqhl,blhd->bqhd', p[..., m:], v_loc))


def moba_prompt(q, k, v, slopes):
    B, T, H, Dh = q.shape
    n_full = T // MOBA_BLOCK
    n_sel = min(MOBA_TOPK, n_full)
    t_pad = -(-T // MOBA_BLOCK) * MOBA_BLOCK
    pad = ((0, 0), (0, t_pad - T), (0, 0), (0, 0))
    k_pad, v_pad = jnp.pad(k, pad), jnp.pad(v, pad)
    if n_sel > 0:
        to_blocks = lambda a: a[:, :n_full * MOBA_BLOCK].reshape(B, n_full, MOBA_BLOCK, H, Dh).transpose(0, 1, 3, 2, 4)
        k_blk, v_blk = to_blocks(k), to_blocks(v)
        k_mean = jnp.mean(k_blk.astype(F32), axis=3)
        gate = jnp.einsum('bthd,bjhd->bthj', q.astype(F32), k_mean)
        past = jnp.arange(n_full)[None, :] < (jnp.arange(T) // MOBA_BLOCK)[:, None]
        gate = jnp.where(past[None, :, None, :], gate, -jnp.inf)
        top_score, sel = lax.top_k(gate, n_sel)
        sel_ok = jnp.isfinite(top_score)
    b_idx = jnp.arange(B)[:, None, None, None]
    h_idx = jnp.arange(H)[None, None, :, None]
    offs = jnp.arange(MOBA_BLOCK)

    def one_chunk(c):
        q0 = c * Q_CHUNK
        qc = lax.dynamic_slice_in_dim(q, q0, Q_CHUNK, axis=1)
        qpos = q0 + jnp.arange(Q_CHUNK)
        b0 = (q0 // MOBA_BLOCK) * MOBA_BLOCK
        k_loc = lax.dynamic_slice_in_dim(k_pad, b0, MOBA_BLOCK, axis=1)
        v_loc = lax.dynamic_slice_in_dim(v_pad, b0, MOBA_BLOCK, axis=1)
        pos_loc = b0 + offs
        if n_sel == 0:
            return moba_core(qc, k_loc, v_loc, pos_loc, qpos, slopes)
        sc = lax.dynamic_slice_in_dim(sel, q0, Q_CHUNK, axis=1)
        okc = lax.dynamic_slice_in_dim(sel_ok, q0, Q_CHUNK, axis=1)
        m = n_sel * MOBA_BLOCK
        k_sel = k_blk[b_idx, sc, h_idx].reshape(B, Q_CHUNK, H, m, Dh)
        v_sel = v_blk[b_idx, sc, h_idx].reshape(B, Q_CHUNK, H, m, Dh)
        pos_sel = (sc[..., None] * MOBA_BLOCK + offs).reshape(B, Q_CHUNK, H, m)
        ok = jnp.broadcast_to(okc[..., None], sc.shape + (MOBA_BLOCK,)).reshape(B, Q_CHUNK, H, m)
        return moba_core(qc, k_loc, v_loc, pos_loc, qpos, slopes, (k_sel, v_sel, pos_sel, ok))

    out = lax.map(one_chunk, jnp.arange(T // Q_CHUNK))
    return out.transpose(1, 0, 2, 3, 4).reshape(B, T, H, Dh)


def moba_sample(q, k, v, *, cache_k, cache_v, page_table, layer, slopes):
    DB, S, H, Dh = q.shape
    ppb = MOBA_BLOCK // PAGE_SIZE
    n_full = PAST_LEN // MOBA_BLOCK
    n_sel = min(MOBA_TOPK, n_full)
    rem_pages = (PAST_LEN - n_full * MOBA_BLOCK) // PAGE_SIZE
    qpos = PAST_LEN + jnp.arange(S)
    rows = jnp.arange(PAGE_SIZE)
    own = page_table[:, n_full * ppb:n_full * ppb + rem_pages]
    k_loc = jnp.concatenate([cache_k[layer, own].reshape(DB, rem_pages * PAGE_SIZE, H, Dh), k], axis=1)
    v_loc = jnp.concatenate([cache_v[layer, own].reshape(DB, rem_pages * PAGE_SIZE, H, Dh), v], axis=1)
    pos_loc = jnp.concatenate([n_full * MOBA_BLOCK + jnp.arange(rem_pages * PAGE_SIZE), qpos])
    if n_sel == 0:
        return moba_core(q, k_loc, v_loc, pos_loc, qpos, slopes)
    past_k = cache_k[layer, page_table[:, :n_full * ppb]]
    k_mean = jnp.mean(past_k.astype(F32).reshape(DB, n_full, MOBA_BLOCK, H, Dh), axis=2)
    gate = jnp.einsum('bshd,bjhd->bshj', q.astype(F32), k_mean)
    _, sel = lax.top_k(gate, n_sel)
    lpage = sel[..., None] * ppb + jnp.arange(ppb)
    phys = page_table[jnp.arange(DB)[:, None, None, None, None], lpage]
    h_idx = jnp.arange(H)[:, None, None, None]
    m = n_sel * MOBA_BLOCK
    k_sel = cache_k[layer, phys[..., None], rows, h_idx].reshape(DB, S, H, m, Dh)
    v_sel = cache_v[layer, phys[..., None], rows, h_idx].reshape(DB, S, H, m, Dh)
    pos_sel = (lpage[..., None] * PAGE_SIZE + rows).reshape(DB, S, H, m)
    ok = jnp.ones(pos_sel.shape, bool)
    return moba_core(q, k_loc, v_loc, pos_loc, qpos, slopes, (k_sel, v_sel, pos_sel, ok))


def mixing_sublayer(x, s0, attend, *, norm_g, w_in_l, lb, g_out, qn, kn, wa, wb, wo):
    B, T, _ = x.shape
    h = rmsnorm(x, norm_g)
    zq, zf, zi, zg, aq, ak, av, ga, gb = split_in(h @ w_in_l)
    o_a, s_new = hgrn2_mixer(zq, zf, zi, zg, lb, g_out, s0)
    q = rmsnorm(aq.reshape(B, T, N_HEADS, HEAD_DIM), qn)
    k = rmsnorm(ak.reshape(B, T, KV_HEADS, HEAD_DIM), kn)
    v = av.reshape(B, T, KV_HEADS, HEAD_DIM)
    o_b = attend(q, k, v).reshape(B, T, ATT_WIDTH)
    merged = jax.nn.sigmoid(ga) * (o_a @ wa) + jax.nn.sigmoid(gb) * (o_b @ wb)
    return x + merged @ wo, k, v, s_new


def swiglu(x, wg, wu, wd):
    return (jax.nn.silu(x @ wg) * (x @ wu)) @ wd


def moe_rows_per_block(n_slots):
    r = 8
    while r < MOE_MAX_ROWS and r * N_EXPERTS < n_slots:
        r *= 2
    return r


def moe_swiglu(x2d, w_router, w_gate, w_up, w_down):
    N, D = x2d.shape
    logits = (x2d @ w_router).astype(F32)
    top_l, top_e = lax.top_k(logits, TOP_K)
    gates = jax.nn.softmax(top_l, axis=-1).astype(x2d.dtype)
    n_slots = N * TOP_K
    R = moe_rows_per_block(n_slots)
    n_blocks = -(-(n_slots + N_EXPERTS * (R - 1)) // R)
    flat_e = top_e.reshape(-1)
    order = jnp.argsort(flat_e)
    sorted_e = flat_e[order]
    counts = jnp.bincount(flat_e, length=N_EXPERTS)
    padded = (counts + R - 1) // R * R
    start_sorted = jnp.cumsum(counts) - counts
    end_pad = jnp.cumsum(padded)
    start_pad = end_pad - padded
    dest = start_pad[sorted_e] + jnp.arange(n_slots) - start_sorted[sorted_e]
    row_tok = jnp.full((n_blocks * R,), N, jnp.int32).at[dest].set((order // TOP_K).astype(jnp.int32))
    x_ext = jnp.concatenate([x2d, jnp.zeros((1, D), x2d.dtype)], axis=0)
    xb = x_ext[row_tok].reshape(n_blocks, R, D)
    blk_e = jnp.minimum(jnp.searchsorted(end_pad, jnp.arange(n_blocks) * R, side='right'), N_EXPERTS - 1)

    def expert_block(args):
        xr, e = args
        return swiglu(xr, w_gate[e], w_up[e], w_down[e])

    yb = lax.map(expert_block, (xb, blk_e)).reshape(n_blocks * R, D)
    slot_row = jnp.zeros((n_slots,), dest.dtype).at[order].set(dest)
    return jnp.sum(yb[slot_row].reshape(N, TOP_K, D) * gates[..., None], axis=1)


def channel_sublayer(x, *, norm_g, layer, wdg, wdu, wdd, wr, weg, weu, wed):
    B, T, D = x.shape
    h = rmsnorm(x, norm_g)
    i = layer // 2
    if layer % 2 == 0:
        y = swiglu(h, wdg[i], wdu[i], wdd[i])
    else:
        y = moe_swiglu(h.reshape(B * T, D), wr[i], weg[i], weu[i], wed[i]).reshape(B, T, D)
    return x + y


def setup_inputs(seed: int = 0) -> dict:
    key = jax.random.key(seed)
    ks = jax.random.split(key, 24)
    n_pages = PAST_LEN // PAGE_SIZE
    n_pool = (5 * DEC_BATCH * n_pages + 3) // 4
    nrm = lambda k, shape, s: jax.random.normal(k, shape, F32) * s
    gain = lambda k, shape: 1.0 + 0.02 * jax.random.normal(k, shape, F32)
    page_table = jax.random.permutation(ks[5], n_pool)[:DEC_BATCH * n_pages].reshape(DEC_BATCH, n_pages).astype(jnp.int32)
    return {
        'x_prompt': nrm(ks[0], (BATCH, SEQ, D_MODEL), 1.0),
        'x_sample': nrm(ks[1], (DEC_BATCH, DEC_SEQ, D_MODEL), 1.0),
        'cache_k': nrm(ks[2], (DEPTH, n_pool, PAGE_SIZE, KV_HEADS, HEAD_DIM), 1.0),
        'cache_v': nrm(ks[3], (DEPTH, n_pool, PAGE_SIZE, KV_HEADS, HEAD_DIM), 1.0),
        'state_hgrn': nrm(ks[4], (DEPTH, DEC_BATCH, HG_HEADS, HG_HEAD_DIM, HG_HEAD_DIM), 0.5),
        'page_table': page_table,
        'w_in': nrm(ks[6], (DEPTH, D_MODEL, IN_WIDTH), D_MODEL ** -0.5),
        'hg_lower': nrm(ks[7], (DEPTH, HG_WIDTH), 0.5),
        'hg_norm': gain(ks[8], (DEPTH, HG_HEAD_DIM)),
        'q_norm': gain(ks[9], (DEPTH, HEAD_DIM)),
        'k_norm': gain(ks[10], (DEPTH, HEAD_DIM)),
        'w_branch_a': nrm(ks[11], (DEPTH, HG_WIDTH, D_MODEL), HG_WIDTH ** -0.5),
        'w_branch_b': nrm(ks[12], (DEPTH, ATT_WIDTH, D_MODEL), ATT_WIDTH ** -0.5),
        'w_out': nrm(ks[13], (DEPTH, D_MODEL, D_MODEL), D_MODEL ** -0.5),
        'attn_norm': gain(ks[14], (DEPTH, D_MODEL)),
        'ffn_norm': gain(ks[15], (DEPTH, D_MODEL)),
        'w_dense_gate': nrm(ks[16], (N_DENSE, D_MODEL, D_FF), D_MODEL ** -0.5),
        'w_dense_up': nrm(ks[17], (N_DENSE, D_MODEL, D_FF), D_MODEL ** -0.5),
        'w_dense_down': nrm(ks[18], (N_DENSE, D_FF, D_MODEL), D_FF ** -0.5),
        'w_router': nrm(ks[19], (N_MOE, D_MODEL, N_EXPERTS), D_MODEL ** -0.5),
        'w_exp_gate': nrm(ks[20], (N_MOE, N_EXPERTS, D_MODEL, D_FF), D_MODEL ** -0.5),
        'w_exp_up': nrm(ks[21], (N_MOE, N_EXPERTS, D_MODEL, D_FF), D_MODEL ** -0.5),
        'w_exp_down': nrm(ks[22], (N_MOE, N_EXPERTS, D_FF, D_MODEL), D_FF ** -0.5),
    }


def reference(x_prompt, x_sample, cache_k, cache_v, state_hgrn, page_table, w_in, hg_lower, hg_norm, q_norm, k_norm, w_branch_a, w_branch_b, w_out, attn_norm, ffn_norm, w_dense_gate, w_dense_up, w_dense_down, w_router, w_exp_gate, w_exp_up, w_exp_down):
    slopes = alibi_slopes()
    lb_cum = jnp.cumsum(jax.nn.softmax(hg_lower.astype(F32), axis=0), axis=0)
    lower_bounds = lb_cum - lb_cum[:1]
    xp, xs = x_prompt, x_sample
    kp_rows, vp_rows, sp_fin, ks_rows, vs_rows, ss_fin = [], [], [], [], [], []
    for l in range(DEPTH):
        mix = functools.partial(mixing_sublayer, norm_g=attn_norm[l], w_in_l=w_in[l], lb=lower_bounds[l], g_out=hg_norm[l], qn=q_norm[l], kn=k_norm[l], wa=w_branch_a[l], wb=w_branch_b[l], wo=w_out[l])
        s0p = jnp.zeros((xp.shape[0], HG_HEADS, HG_HEAD_DIM, HG_HEAD_DIM), xp.dtype)
        xp, kp, vp, sp = mix(xp, s0p, functools.partial(moba_prompt, slopes=slopes))
        xs, kn_, vn_, sn_ = mix(xs, state_hgrn[l], functools.partial(moba_sample, cache_k=cache_k, cache_v=cache_v, page_table=page_table, layer=l, slopes=slopes))
        ffn = functools.partial(channel_sublayer, norm_g=ffn_norm[l], layer=l, wdg=w_dense_gate, wdu=w_dense_up, wdd=w_dense_down, wr=w_router, weg=w_exp_gate, weu=w_exp_up, wed=w_exp_down)
        xp = ffn(xp)
        xs = ffn(xs)
        kp_rows.append(kp); vp_rows.append(vp); sp_fin.append(sp)
        ks_rows.append(kn_); vs_rows.append(vn_); ss_fin.append(sn_)
    return (xp, xs, jnp.stack(kp_rows), jnp.stack(vp_rows), jnp.stack(sp_fin), jnp.stack(ks_rows), jnp.stack(vs_rows), jnp.stack(ss_fin))
```

```python
import functools

import numpy as np
import jax
import jax.numpy as jnp
from jax import lax
from jax.experimental import pallas as pl
from jax.experimental.pallas import tpu as pltpu

F32 = jnp.float32
BF16 = jnp.bfloat16

D_MODEL = 1024
HG_WIDTH = 512
HG_HEAD_DIM = 128
HG_HEADS = HG_WIDTH // HG_HEAD_DIM
HG_CHUNK = 16
HEAD_DIM = 64
N_HEADS = 8
ATT_WIDTH = N_HEADS * HEAD_DIM
MOBA_BLOCK = 256
MOBA_TOPK = 3
PAGE_SIZE = 128
PAGES_PER_BLOCK = MOBA_BLOCK // PAGE_SIZE
D_FF = 2816
N_EXPERTS = 8
TOP_K = 2
RMS_EPS = 1e-6
IN_WIDTH = 4 * HG_WIDTH + 3 * ATT_WIDTH + 2 * D_MODEL
COL_ZQ, COL_ZF, COL_ZI, COL_ZG, COL_AQ, COL_AK, COL_AV = range(7)
COL_GA, COL_GB = 7, 9

MASKED = -1e30
VMEM_LIMIT = 56 * 1024 * 1024
FF_TILE = D_FF // 2
MOE_ROWS = 512
PAGE_GROUP = 16
LANES = 128


def _cparams(sem):
    return pltpu.CompilerParams(dimension_semantics=sem, vmem_limit_bytes=VMEM_LIMIT)


def _row_tile(n, hi):
    for t in range(hi, hi // 4, -16):
        if n % t == 0:
            return t
    return hi


def _split3(a):
    a1 = a.astype(BF16)
    r = a - a1.astype(F32)
    a2 = r.astype(BF16)
    a3 = (r - a2.astype(F32)).astype(BF16)
    return a1, a2, a3


def _dot(a, b):
    return jnp.dot(a, b, preferred_element_type=F32)


def _dot_nt(a, b):
    return lax.dot_general(a, b, (((1,), (1,)), ((), ())), preferred_element_type=F32)


def _dot_f32_lhs(a, b_exact):
    a1, a2, a3 = _split3(a)
    return _dot(a1, b_exact) + _dot(a2, b_exact) + _dot(a3, b_exact)


def _dot_f32_rhs(a_exact, b):
    b1, b2, b3 = _split3(b)
    return _dot(a_exact, b1) + _dot(a_exact, b2) + _dot(a_exact, b3)


def _dot_f32(a, b):
    a1, a2, a3 = _split3(a)
    b1, b2, b3 = _split3(b)
    return (_dot(a1, b1) + (_dot(a1, b2) + _dot(a2, b1))
            + (_dot(a1, b3) + _dot(a2, b2) + _dot(a3, b1)))


def _rms(x, g):
    return x * lax.rsqrt(jnp.mean(x * x, axis=-1, keepdims=True) + RMS_EPS) * g


def _silu(x):
    return x / (1.0 + jnp.exp(-x))


def _sigmoid(x):
    return 1.0 / (1.0 + jnp.exp(-x))


def _top_mask_axis0(g, k):
    idx = lax.broadcasted_iota(jnp.int32, g.shape, 0)
    sel = jnp.zeros(g.shape, jnp.bool_)
    picks = []
    for _ in range(k):
        mx = jnp.max(g, axis=0, keepdims=True)
        first = jnp.min(jnp.where(g == mx, idx, g.shape[0]), axis=0, keepdims=True)
        hit = (idx == first) & (mx > -jnp.inf)
        sel = sel | hit
        g = jnp.where(idx == first, -jnp.inf, g)
        picks.append(first)
    return sel, picks


def _proj_kernel(x_ref, g_ref, w_ref, o_ref, h_scr):
    @pl.when(pl.program_id(1) == 0)
    def _():
        h_scr[...] = _rms(x_ref[...], g_ref[...]).astype(BF16)

    o_ref[...] = _dot(h_scr[...], w_ref[...])


def _proj(x, g, w_bf16):
    n = x.shape[0]
    tm = _row_tile(n, 1024)
    tn = IN_WIDTH // 4
    return pl.pallas_call(
        _proj_kernel,
        grid=(pl.cdiv(n, tm), IN_WIDTH // tn),
        in_specs=[pl.BlockSpec((tm, D_MODEL), lambda i, j: (i, 0)),
                  pl.BlockSpec((1, D_MODEL), lambda i, j: (0, 0)),
                  pl.BlockSpec((D_MODEL, tn), lambda i, j: (0, j))],
        out_specs=pl.BlockSpec((tm, tn), lambda i, j: (i, j)),
        out_shape=jax.ShapeDtypeStruct((n, IN_WIDTH), F32),
        scratch_shapes=[pltpu.VMEM((tm, D_MODEL), BF16)],
        compiler_params=_cparams(("parallel", "arbitrary")),
        name="in_proj",
    )(x, g.reshape(1, D_MODEL), w_bf16)


def _group_rms(x, gain, gmat):
    ms = _dot_f32_lhs(x * x, gmat) * (1.0 / HEAD_DIM)
    return x * lax.rsqrt(ms + RMS_EPS) * gain


def _qk_prompt_kernel(aq_ref, ak_ref, av_ref, qn_ref, kn_ref, gm_ref,
                      k_out, v_out, qt_out, kh_out, vt_out, km_out):
    gmat = gm_ref[...]
    q = _group_rms(aq_ref[...], qn_ref[...], gmat)
    k = _group_rms(ak_ref[...], kn_ref[...], gmat)
    v = av_ref[...]
    k_out[...] = k
    v_out[...] = v
    qt_out[0, :, 0] = q.T.reshape(N_HEADS, HEAD_DIM, MOBA_BLOCK)
    vt_out[0, :, 0] = v.T.reshape(N_HEADS, HEAD_DIM, MOBA_BLOCK).astype(BF16)
    kb = k.astype(BF16)
    for h in range(N_HEADS):
        kh_out[0, h, 0] = kb[:, h * HEAD_DIM:(h + 1) * HEAD_DIM]
    km_out[0, 0] = jnp.sum(k, axis=0, keepdims=True) * (1.0 / MOBA_BLOCK)


def _qk_prompt(z, qn, kn, gmat, nb, nblk):
    rows = nb * nblk * MOBA_BLOCK
    zspec = lambda c: pl.BlockSpec((MOBA_BLOCK, ATT_WIDTH), lambda b, i, c=c: (b * nblk + i, c))
    vec = pl.BlockSpec((1, ATT_WIDTH), lambda b, i: (0, 0))
    head5 = lambda r, c: pl.BlockSpec((1, N_HEADS, 1, r, c), lambda b, i: (b, 0, i, 0, 0))
    return pl.pallas_call(
        _qk_prompt_kernel,
        grid=(nb, nblk),
        in_specs=[zspec(COL_AQ), zspec(COL_AK), zspec(COL_AV), vec, vec,
                  pl.BlockSpec((ATT_WIDTH, ATT_WIDTH), lambda b, i: (0, 0))],
        out_specs=[pl.BlockSpec((MOBA_BLOCK, ATT_WIDTH), lambda b, i: (b * nblk + i, 0)),
                   pl.BlockSpec((MOBA_BLOCK, ATT_WIDTH), lambda b, i: (b * nblk + i, 0)),
                   head5(HEAD_DIM, MOBA_BLOCK), head5(MOBA_BLOCK, HEAD_DIM), head5(HEAD_DIM, MOBA_BLOCK),
                   pl.BlockSpec((1, 1, 1, ATT_WIDTH), lambda b, i: (b, i, 0, 0))],
        out_shape=[jax.ShapeDtypeStruct((rows, ATT_WIDTH), F32),
                   jax.ShapeDtypeStruct((rows, ATT_WIDTH), F32),
                   jax.ShapeDtypeStruct((nb, N_HEADS, nblk, HEAD_DIM, MOBA_BLOCK), F32),
                   jax.ShapeDtypeStruct((nb, N_HEADS, nblk, MOBA_BLOCK, HEAD_DIM), BF16),
                   jax.ShapeDtypeStruct((nb, N_HEADS, nblk, HEAD_DIM, MOBA_BLOCK), BF16),
                   jax.ShapeDtypeStruct((nb, nblk, 1, ATT_WIDTH), F32)],
        compiler_params=_cparams(("parallel", "parallel")),
        name="qk_prompt",
    )(z, z, z, qn, kn, gmat)


def _qk_sample_kernel(aq_ref, ak_ref, av_ref, qn_ref, kn_ref, gm_ref, q_out, k_out, v_out):
    gmat = gm_ref[...]
    q_out[...] = _group_rms(aq_ref[...], qn_ref[...], gmat)
    k_out[...] = _group_rms(ak_ref[...], kn_ref[...], gmat)
    v_out[...] = av_ref[...]


def _qk_sample(z, qn, kn, gmat, row_blk, db):
    zspec = lambda c: pl.BlockSpec((db, ATT_WIDTH), lambda i, c=c: (row_blk, c))
    vec = pl.BlockSpec((1, ATT_WIDTH), lambda i: (0, 0))
    out = pl.BlockSpec((db, ATT_WIDTH), lambda i: (0, 0))
    shp = jax.ShapeDtypeStruct((db, ATT_WIDTH), F32)
    return pl.pallas_call(
        _qk_sample_kernel,
        grid=(1,),
        in_specs=[zspec(COL_AQ), zspec(COL_AK), zspec(COL_AV), vec, vec,
                  pl.BlockSpec((ATT_WIDTH, ATT_WIDTH), lambda i: (0, 0))],
        out_specs=[out, out, out],
        out_shape=[shp, shp, shp],
        compiler_params=_cparams(("arbitrary",)),
        name="qk_sample",
    )(z, z, z, qn, kn, gmat)


def _moba_prompt_kernel(slopes_ref, qt_ref, kh_ref, vt_ref, km_ref, o_ref, sel_scr, bias_scr):
    h = pl.program_id(1)
    i = pl.program_id(2)
    nblk = km_ref.shape[2]
    slope = slopes_ref[h]
    qt = qt_ref[0, 0, 0]

    gate = _dot_f32(km_ref[0, 0], qt)
    blk = lax.broadcasted_iota(jnp.int32, gate.shape, 0)
    sel, _ = _top_mask_axis0(jnp.where(blk < i, gate, -jnp.inf), MOBA_TOPK)
    sel_scr[...] = jnp.where(sel, 0.0, MASKED)

    key = lax.broadcasted_iota(jnp.int32, (MOBA_BLOCK, MOBA_BLOCK), 0)
    qry = lax.broadcasted_iota(jnp.int32, (MOBA_BLOCK, MOBA_BLOCK), 1)
    alibi = slope * (key - qry).astype(F32)
    bias_scr[...] = alibi

    qb = (qt * (HEAD_DIM ** -0.5)).astype(BF16)

    s = _dot(kh_ref[0, 0, i], qb) + jnp.where(key <= qry, alibi, MASKED)
    m = jnp.max(s, axis=0, keepdims=True)
    p = jnp.exp(s - m)
    l = jnp.sum(p, axis=0, keepdims=True)
    acc = _dot(vt_ref[0, 0, i], p.astype(BF16))

    def body(j, carry):
        m, l, acc = carry
        row = sel_scr[pl.ds(j, 1), :] - slope * ((i - j) * MOBA_BLOCK).astype(F32)
        s = _dot(kh_ref[0, 0, j], qb) + bias_scr[...] + row
        m_new = jnp.maximum(m, jnp.max(s, axis=0, keepdims=True))
        alpha = jnp.exp(m - m_new)
        p = jnp.exp(s - m_new)
        l = alpha * l + jnp.sum(p, axis=0, keepdims=True)
        acc = alpha * acc + _dot(vt_ref[0, 0, j], p.astype(BF16))
        return m_new, l, acc

    m, l, acc = lax.fori_loop(0, i, body, (m, l, acc))
    o_ref[0, 0] = acc / l


def _moba_prompt(slopes, qt, kh, vt, kmh):
    nb, _, nblk = qt.shape[:3]
    return pl.pallas_call(
        _moba_prompt_kernel,
        grid_spec=pltpu.PrefetchScalarGridSpec(
            num_scalar_prefetch=1,
            grid=(nb, N_HEADS, nblk),
            in_specs=[pl.BlockSpec((1, 1, 1, HEAD_DIM, MOBA_BLOCK), lambda b, h, i, s: (b, h, i, 0, 0)),
                      pl.BlockSpec((1, 1, nblk, MOBA_BLOCK, HEAD_DIM), lambda b, h, i, s: (b, h, 0, 0, 0)),
                      pl.BlockSpec((1, 1, nblk, HEAD_DIM, MOBA_BLOCK), lambda b, h, i, s: (b, h, 0, 0, 0)),
                      pl.BlockSpec((1, 1, nblk, HEAD_DIM), lambda b, h, i, s: (b, h, 0, 0))],
            out_specs=pl.BlockSpec((1, 1, HEAD_DIM, MOBA_BLOCK), lambda b, h, i, s: (b, h, 0, i)),
            scratch_shapes=[pltpu.VMEM((nblk, MOBA_BLOCK), F32),
                            pltpu.VMEM((MOBA_BLOCK, MOBA_BLOCK), F32)]),
        out_shape=jax.ShapeDtypeStruct((nb, N_HEADS, HEAD_DIM, nblk * MOBA_BLOCK), F32),
        compiler_params=_cparams(("parallel", "parallel", "arbitrary")),
        name="moba_prompt",
    )(slopes, qt, kh, vt, kmh)


def _hgrn_gates(zf, lb):
    log_sig = jnp.minimum(zf, 0.0) - jnp.log(1.0 + jnp.exp(-jnp.abs(zf)))
    a = jnp.log(lb)
    b = jnp.log(1.0 - lb) + log_sig
    logf = jnp.maximum(a, b) + jnp.log(1.0 + jnp.exp(-jnp.abs(a - b)))
    k = (1.0 - lb) / (1.0 + jnp.exp(zf))
    return logf, k


def _hgrn_prompt_kernel(zq_ref, zf_ref, zi_ref, zg_ref, lb_ref, go_ref, tri_ref, upper_ref,
                        o_ref, s_ref, st_scr, oi_scr):
    i = pl.program_id(2)
    tb = zq_ref.shape[0]

    @pl.when(i == 0)
    def _():
        st_scr[...] = jnp.zeros_like(st_scr)

    zq = zq_ref[...]
    v = zi_ref[...]
    logf, k = _hgrn_gates(zf_ref[...], lb_ref[...])
    tri = tri_ref[...]
    b = _dot_f32_rhs(tri, logf)
    rest = _dot_f32_rhs(upper_ref[...], logf)
    q_in = (zq * jnp.exp(b)).astype(BF16)
    k_in = (k * jnp.exp(-b)).astype(BF16)
    k_out = k * jnp.exp(rest)
    dec = jnp.exp(b + rest)
    vb = v.astype(BF16)

    att = jnp.where(tri > 0, _dot_nt(q_in, k_in), 0.0).astype(BF16)
    o_intra = _dot(att, vb)

    vt = v.T.astype(BF16)
    chunk = lax.broadcasted_iota(jnp.int32, (tb, HG_HEAD_DIM), 0) // HG_CHUNK
    st = st_scr[...]
    for n in range(tb // HG_CHUNK):
        rows = slice(n * HG_CHUNK, (n + 1) * HG_CHUNK)
        oi_scr[rows, :] = _dot_nt(q_in[rows], st.astype(BF16))
        k_n = jnp.where(chunk == n, k_out, 0.0).astype(BF16)
        st = st * dec[n * HG_CHUNK:n * HG_CHUNK + 1, :] + _dot(vt, k_n)
    st_scr[...] = st

    o = o_intra + oi_scr[...]
    o_ref[...] = _rms(o, go_ref[...]) * _silu(zg_ref[...])

    @pl.when(i == pl.num_programs(2) - 1)
    def _():
        s_ref[0, 0] = st.T


def _hgrn_prompt(z, lb, g_out, tri, upper, nb, seq):
    tb = MOBA_BLOCK
    nt = seq // tb
    zspec = lambda c: pl.BlockSpec((tb, HG_HEAD_DIM), lambda b, h, i, c=c: (b * nt + i, 4 * c + h))
    sq = pl.BlockSpec((tb, tb), lambda b, h, i: (0, 0))
    return pl.pallas_call(
        _hgrn_prompt_kernel,
        grid=(nb, HG_HEADS, nt),
        in_specs=[zspec(COL_ZQ), zspec(COL_ZF), zspec(COL_ZI), zspec(COL_ZG),
                  pl.BlockSpec((1, HG_HEAD_DIM), lambda b, h, i: (0, h)),
                  pl.BlockSpec((1, HG_HEAD_DIM), lambda b, h, i: (0, 0)),
                  sq, sq],
        out_specs=[pl.BlockSpec((tb, HG_HEAD_DIM), lambda b, h, i: (b * nt + i, h)),
                   pl.BlockSpec((1, 1, HG_HEAD_DIM, HG_HEAD_DIM), lambda b, h, i: (b, h, 0, 0))],
        out_shape=[jax.ShapeDtypeStruct((nb * seq, HG_WIDTH), F32),
                   jax.ShapeDtypeStruct((nb, HG_HEADS, HG_HEAD_DIM, HG_HEAD_DIM), F32)],
        scratch_shapes=[pltpu.VMEM((HG_HEAD_DIM, HG_HEAD_DIM), F32),
                        pltpu.VMEM((tb, HG_HEAD_DIM), F32)],
        compiler_params=_cparams(("parallel", "parallel", "arbitrary")),
        name="hgrn_prompt",
    )(z, z, z, z, lb, g_out, tri, upper)


def _hgrn_sample_kernel(zq_ref, zf_ref, zi_ref, zg_ref, lb_ref, go_ref, s_ref, o_ref, sn_ref):
    db = zq_ref.shape[0]
    eye = (lax.broadcasted_iota(jnp.int32, (HG_HEAD_DIM, HG_HEAD_DIM), 0)
           == lax.broadcasted_iota(jnp.int32, (HG_HEAD_DIM, HG_HEAD_DIM), 1))
    col = lambda r: jnp.sum(jnp.where(eye, r, 0.0), axis=1, keepdims=True)
    lb = lb_ref[...]
    go = go_ref[...]

    def body(b, carry):
        row = pl.ds(b, 1)
        logf, k = _hgrn_gates(zf_ref[row, :], lb)
        s_new = s_ref[0, b, 0] * col(jnp.exp(logf)) + col(k) * zi_ref[row, :]
        sn_ref[b, 0] = s_new
        o = jnp.sum(s_new * col(zq_ref[row, :]), axis=0, keepdims=True)
        o_ref[row, :] = _rms(o, go) * _silu(zg_ref[row, :])
        return carry

    lax.fori_loop(0, db, body, 0)


def _hgrn_sample(z, lb, g_out, state, layer, row_blk, db):
    zspec = lambda c: pl.BlockSpec((db, HG_HEAD_DIM), lambda h, c=c: (row_blk, 4 * c + h))
    return pl.pallas_call(
        _hgrn_sample_kernel,
        grid=(HG_HEADS,),
        in_specs=[zspec(COL_ZQ), zspec(COL_ZF), zspec(COL_ZI), zspec(COL_ZG),
                  pl.BlockSpec((1, HG_HEAD_DIM), lambda h: (0, h)),
                  pl.BlockSpec((1, HG_HEAD_DIM), lambda h: (0, 0)),
                  pl.BlockSpec((1, db, 1, HG_HEAD_DIM, HG_HEAD_DIM), lambda h: (layer, 0, h, 0, 0))],
        out_specs=[pl.BlockSpec((db, HG_HEAD_DIM), lambda h: (0, h)),
                   pl.BlockSpec((db, 1, HG_HEAD_DIM, HG_HEAD_DIM), lambda h: (0, h, 0, 0))],
        out_shape=[jax.ShapeDtypeStruct((db, HG_WIDTH), F32),
                   jax.ShapeDtypeStruct((db, HG_HEADS, HG_HEAD_DIM, HG_HEAD_DIM), F32)],
        compiler_params=_cparams(("parallel",)),
        name="hgrn_sample",
    )(z, z, z, z, lb, g_out, state)


def _cache_mean_kernel(pt_ref, *refs):
    pages, o_ref = refs[:-1], refs[-1]
    g = pl.program_id(1)
    per_step = len(pages) // PAGES_PER_BLOCK
    for c in range(per_step):
        tot = jnp.sum(pages[PAGES_PER_BLOCK * c][0], axis=0, keepdims=True)
        for p in range(1, PAGES_PER_BLOCK):
            tot = tot + jnp.sum(pages[PAGES_PER_BLOCK * c + p][0], axis=0, keepdims=True)
        o_ref[0, pl.ds(g * per_step + c, 1), :] = tot * (1.0 / MOBA_BLOCK)


def _cache_mean(page_table, cache2, layer_base, db, n_blk):
    n_pages = n_blk * PAGES_PER_BLOCK
    pg = PAGE_GROUP if n_pages % PAGE_GROUP == 0 else PAGES_PER_BLOCK
    spec = lambda k: pl.BlockSpec((1, PAGE_SIZE, ATT_WIDTH),
                                  lambda b, g, pt, k=k: (layer_base + pt[b, g * pg + k], 0, 0))
    return pl.pallas_call(
        _cache_mean_kernel,
        grid_spec=pltpu.PrefetchScalarGridSpec(
            num_scalar_prefetch=1,
            grid=(db, n_pages // pg),
            in_specs=[spec(k) for k in range(pg)],
            out_specs=pl.BlockSpec((1, n_blk, ATT_WIDTH), lambda b, g, pt: (b, 0, 0))),
        out_shape=jax.ShapeDtypeStruct((db, n_blk, ATT_WIDTH), F32),
        compiler_params=_cparams(("parallel", "arbitrary")),
        name="cache_block_mean",
    )(page_table, *([cache2] * pg))


def _sample_select_kernel(q_ref, km_ref, hm_ref, o_ref):
    b = pl.program_id(0)
    prod = km_ref[0] * q_ref[pl.ds(b, 1), :]
    gate = _dot_f32_lhs(prod, hm_ref[...])
    _, picks = _top_mask_axis0(gate, MOBA_TOPK)
    rows = picks + [jnp.zeros_like(picks[0])] * (o_ref.shape[1] - len(picks))
    o_ref[0] = jnp.concatenate(rows, axis=0)


def _sample_select(q_s, km_s, head_mat):
    db, n_blk, _ = km_s.shape
    return pl.pallas_call(
        _sample_select_kernel,
        grid=(db,),
        in_specs=[pl.BlockSpec((db, ATT_WIDTH), lambda b: (0, 0)),
                  pl.BlockSpec((1, n_blk, ATT_WIDTH), lambda b: (b, 0, 0)),
                  pl.BlockSpec((ATT_WIDTH, LANES), lambda b: (0, 0))],
        out_specs=pl.BlockSpec((1, 8, LANES), lambda b: (b, 0, 0)),
        out_shape=jax.ShapeDtypeStruct((db, 8, LANES), jnp.int32),
        compiler_params=_cparams(("parallel",)),
        name="sample_select",
    )(q_s, km_s, head_mat)


def _moba_sample_kernel(past_len, phys_ref, blk_ref, q_ref, k_ref, v_ref, *refs):
    n_sel_pages = N_HEADS * MOBA_TOPK * PAGES_PER_BLOCK
    k_pages, v_pages, o_ref = refs[:n_sel_pages], refs[n_sel_pages:2 * n_sel_pages], refs[-1]
    b = pl.program_id(0)
    row = pl.ds(b, 1)
    m_keys = MOBA_TOPK * MOBA_BLOCK
    lane = lax.broadcasted_iota(jnp.int32, (1, m_keys), 1)
    q_row, k_row, v_row = q_ref[row, :], k_ref[row, :], v_ref[row, :]
    outs = []
    for h in range(N_HEADS):
        lo = (h % 2) * HEAD_DIM
        cols = slice(h * HEAD_DIM, (h + 1) * HEAD_DIM)
        first = h * MOBA_TOPK * PAGES_PER_BLOCK
        kc = jnp.concatenate([k_pages[first + t][0][:, lo:lo + HEAD_DIM]
                              for t in range(MOBA_TOPK * PAGES_PER_BLOCK)], axis=0).astype(BF16)
        vc = jnp.concatenate([v_pages[first + t][0][:, lo:lo + HEAD_DIM]
                              for t in range(MOBA_TOPK * PAGES_PER_BLOCK)], axis=0).astype(BF16)
        q = q_row[:, cols] * (HEAD_DIM ** -0.5)
        s = _dot_nt(jnp.broadcast_to(q, (8, HEAD_DIM)).astype(BF16), kc)[0:1]
        kblk = blk_ref[b, h * MOBA_TOPK + MOBA_TOPK - 1]
        for t in range(MOBA_TOPK - 2, -1, -1):
            kblk = jnp.where(lane < (t + 1) * MOBA_BLOCK, blk_ref[b, h * MOBA_TOPK + t], kblk)
        kpos = kblk * MOBA_BLOCK + lane % MOBA_BLOCK
        slope = 2.0 ** (-8.0 * (h + 1) / N_HEADS)
        s = s - slope * (past_len - kpos).astype(F32)
        s_self = jnp.sum(q * k_row[:, cols], axis=1, keepdims=True)
        m = jnp.maximum(jnp.max(s, axis=1, keepdims=True), s_self)
        p = jnp.exp(s - m)
        p_self = jnp.exp(s_self - m)
        l = jnp.sum(p, axis=1, keepdims=True) + p_self
        o = _dot(jnp.broadcast_to(p, (8, m_keys)).astype(BF16), vc)[0:1] + p_self * v_row[:, cols]
        outs.append(o / l)
    o_ref[0] = jnp.concatenate(outs, axis=1)


def _moba_sample(phys, sel_blk, q_s, k_s, v_s, cache_k2, cache_v2, layer_base, past_len):
    db = q_s.shape[0]
    n_sel_pages = N_HEADS * MOBA_TOPK * PAGES_PER_BLOCK

    def page_spec(idx):
        pair = (idx // (MOBA_TOPK * PAGES_PER_BLOCK)) // 2
        return pl.BlockSpec((1, PAGE_SIZE, LANES),
                            lambda b, ph, sb, idx=idx, pair=pair: (layer_base + ph[b, idx], 0, pair))

    full = pl.BlockSpec((db, ATT_WIDTH), lambda b, ph, sb: (0, 0))
    out = pl.pallas_call(
        functools.partial(_moba_sample_kernel, past_len),
        grid_spec=pltpu.PrefetchScalarGridSpec(
            num_scalar_prefetch=2,
            grid=(db,),
            in_specs=[full, full, full] + [page_spec(t) for t in range(n_sel_pages)] * 2,
            out_specs=pl.BlockSpec((1, 1, ATT_WIDTH), lambda b, ph, sb: (b, 0, 0))),
        out_shape=jax.ShapeDtypeStruct((db, 1, ATT_WIDTH), F32),
        compiler_params=_cparams(("parallel",)),
        name="moba_sample",
    )(phys, sel_blk, q_s, k_s, v_s, *([cache_k2] * n_sel_pages), *([cache_v2] * n_sel_pages))
    return out.reshape(db, ATT_WIDTH)


def _merge_kernel(ob_transposed, x_ref, oa_ref, ob_ref, ga0, ga1, gb0, gb1, wa_ref, wb_ref, wo_ref, o_ref):
    oa = oa_ref[...].astype(BF16)
    ob = ob_ref[0].T if ob_transposed else ob_ref[...]
    a = _dot(oa, wa_ref[...])
    bm = _dot(ob.astype(BF16), wb_ref[...])
    half = D_MODEL // 2
    m0 = _sigmoid(ga0[...]) * a[:, :half] + _sigmoid(gb0[...]) * bm[:, :half]
    m1 = _sigmoid(ga1[...]) * a[:, half:] + _sigmoid(gb1[...]) * bm[:, half:]
    y = _dot(m0.astype(BF16), wo_ref[:half, :]) + _dot(m1.astype(BF16), wo_ref[half:, :])
    o_ref[...] = x_ref[...] + y


def _merge(x, z, o_a, o_b, wa, wb, wo, *, tile, row_blk0, n_tiles, ob_transposed, tiles_per_seq=None):
    rspec = lambda w, c: pl.BlockSpec((tile, w), lambda i, c=c: (row_blk0 + i, c))
    if ob_transposed:
        ob_spec = pl.BlockSpec((1, ATT_WIDTH, tile), lambda i: (i // tiles_per_seq, 0, i % tiles_per_seq))
    else:
        ob_spec = pl.BlockSpec((tile, ATT_WIDTH), lambda i: (i, 0))
    wspec = lambda r: pl.BlockSpec((r, D_MODEL), lambda i: (0, 0))
    return pl.pallas_call(
        functools.partial(_merge_kernel, ob_transposed),
        grid=(n_tiles,),
        in_specs=[rspec(D_MODEL, 0),
                  pl.BlockSpec((tile, HG_WIDTH), lambda i: (i, 0)), ob_spec,
                  rspec(512, COL_GA), rspec(512, COL_GA + 1), rspec(512, COL_GB), rspec(512, COL_GB + 1),
                  wspec(HG_WIDTH), wspec(ATT_WIDTH), wspec(D_MODEL)],
        out_specs=rspec(D_MODEL, 0),
        out_shape=jax.ShapeDtypeStruct(x.shape, F32),
        input_output_aliases={0: 0},
        compiler_params=_cparams(("parallel",)),
        name="merge_t" if ob_transposed else "merge",
    )(x, o_a, o_b, z, z, z, z, wa, wb, wo)


def _ffn_kernel(x_ref, g_ref, wg_ref, wu_ref, wd_ref, o_ref, h_scr, acc_scr):
    j = pl.program_id(1)

    @pl.when(j == 0)
    def _():
        h_scr[...] = _rms(x_ref[...], g_ref[...]).astype(BF16)
        acc_scr[...] = jnp.zeros_like(acc_scr)

    h = h_scr[...]
    a = _silu(_dot(h, wg_ref[...])) * _dot(h, wu_ref[...])
    acc_scr[...] += _dot(a.astype(BF16), wd_ref[...])

    @pl.when(j == pl.num_programs(1) - 1)
    def _():
        o_ref[...] = x_ref[...] + acc_scr[...]


def _ffn_dense(x, g, wg, wu, wd):
    n = x.shape[0]
    tm = _row_tile(n, 512)
    return pl.pallas_call(
        _ffn_kernel,
        grid=(pl.cdiv(n, tm), D_FF // FF_TILE),
        in_specs=[pl.BlockSpec((tm, D_MODEL), lambda i, j: (i, 0)),
                  pl.BlockSpec((1, D_MODEL), lambda i, j: (0, 0)),
                  pl.BlockSpec((D_MODEL, FF_TILE), lambda i, j: (0, j)),
                  pl.BlockSpec((D_MODEL, FF_TILE), lambda i, j: (0, j)),
                  pl.BlockSpec((FF_TILE, D_MODEL), lambda i, j: (j, 0))],
        out_specs=pl.BlockSpec((tm, D_MODEL), lambda i, j: (i, 0)),
        out_shape=jax.ShapeDtypeStruct((n, D_MODEL), F32),
        scratch_shapes=[pltpu.VMEM((tm, D_MODEL), BF16), pltpu.VMEM((tm, D_MODEL), F32)],
        compiler_params=_cparams(("parallel", "arbitrary")),
        name="ffn_dense",
    )(x, g.reshape(1, D_MODEL), wg, wu, wd)


def _router_kernel(x_ref, g_ref, wr_ref, h_ref, logit_ref):
    h = _rms(x_ref[...], g_ref[...])
    h_ref[...] = h.astype(BF16)
    logit_ref[...] = _dot_f32(h, wr_ref[...])


def _router(x, g, wr_pad):
    n = x.shape[0]
    tm = _row_tile(n, 1024)
    return pl.pallas_call(
        _router_kernel,
        grid=(pl.cdiv(n, tm),),
        in_specs=[pl.BlockSpec((tm, D_MODEL), lambda i: (i, 0)),
                  pl.BlockSpec((1, D_MODEL), lambda i: (0, 0)),
                  pl.BlockSpec((D_MODEL, LANES), lambda i: (0, 0))],
        out_specs=[pl.BlockSpec((tm, D_MODEL), lambda i: (i, 0)),
                   pl.BlockSpec((tm, LANES), lambda i: (i, 0))],
        out_shape=[jax.ShapeDtypeStruct((n, D_MODEL), BF16),
                   jax.ShapeDtypeStruct((n, LANES), F32)],
        compiler_params=_cparams(("parallel",)),
        name="moe_router",
    )(x, g.reshape(1, D_MODEL), wr_pad)


def _expert_kernel(be_ref, used_ref, x_ref, wg_ref, wu_ref, wd_ref, o_ref, acc_scr):
    i = pl.program_id(0)
    j = pl.program_id(1)

    @pl.when(j == 0)
    def _():
        acc_scr[...] = jnp.zeros_like(acc_scr)

    @pl.when(used_ref[i] > 0)
    def _():
        x = x_ref[...]
        a = _silu(_dot(x, wg_ref[0])) * _dot(x, wu_ref[0])
        acc_scr[...] += _dot(a.astype(BF16), wd_ref[0])

    @pl.when(j == pl.num_programs(1) - 1)
    def _():
        o_ref[...] = acc_scr[...]


def _experts(blk_e, blk_used, xb, weg, weu, wed):
    rows = xb.shape[0]
    return pl.pallas_call(
        _expert_kernel,
        grid_spec=pltpu.PrefetchScalarGridSpec(
            num_scalar_prefetch=2,
            grid=(rows // MOE_ROWS, D_FF // FF_TILE),
            in_specs=[pl.BlockSpec((MOE_ROWS, D_MODEL), lambda i, j, be, bu: (i, 0)),
                      pl.BlockSpec((1, D_MODEL, FF_TILE), lambda i, j, be, bu: (be[i], 0, j)),
                      pl.BlockSpec((1, D_MODEL, FF_TILE), lambda i, j, be, bu: (be[i], 0, j)),
                      pl.BlockSpec((1, FF_TILE, D_MODEL), lambda i, j, be, bu: (be[i], j, 0))],
            out_specs=pl.BlockSpec((MOE_ROWS, D_MODEL), lambda i, j, be, bu: (i, 0)),
            scratch_shapes=[pltpu.VMEM((MOE_ROWS, D_MODEL), F32)]),
        out_shape=jax.ShapeDtypeStruct((rows, D_MODEL), F32),
        compiler_params=_cparams(("parallel", "arbitrary")),
        name="moe_experts",
    )(blk_e, blk_used, xb, weg, weu, wed)


def _moe(x, g, wr, weg, weu, wed):
    n = x.shape[0]
    wr_pad = jnp.zeros((D_MODEL, LANES), F32).at[:, :N_EXPERTS].set(wr)
    h, logits = _router(x, g, wr_pad)
    top_l, top_e = lax.top_k(logits[:, :N_EXPERTS], TOP_K)
    gates = jax.nn.softmax(top_l, axis=-1)
    n_slots = n * TOP_K
    n_blocks = -(-(n_slots + N_EXPERTS * (MOE_ROWS - 1)) // MOE_ROWS)
    flat_e = top_e.reshape(-1)
    order = jnp.argsort(flat_e)
    sorted_e = flat_e[order]
    counts = jnp.bincount(flat_e, length=N_EXPERTS)
    padded = (counts + MOE_ROWS - 1) // MOE_ROWS * MOE_ROWS
    start_sorted = jnp.cumsum(counts) - counts
    end_pad = jnp.cumsum(padded)
    start_pad = end_pad - padded
    dest = (start_pad[sorted_e] + jnp.arange(n_slots) - start_sorted[sorted_e]).astype(jnp.int32)
    row_tok = jnp.full((n_blocks * MOE_ROWS,), n, jnp.int32).at[dest].set((order // TOP_K).astype(jnp.int32))
    blk_start = jnp.arange(n_blocks) * MOE_ROWS
    blk_e = jnp.minimum(jnp.searchsorted(end_pad, blk_start, side='right'), N_EXPERTS - 1).astype(jnp.int32)
    blk_used = (blk_start < end_pad[-1]).astype(jnp.int32)
    xb = jnp.concatenate([h, jnp.zeros((1, D_MODEL), BF16)], axis=0)[row_tok]
    yb = _experts(blk_e, blk_used, xb, weg, weu, wed)
    slot_row = jnp.zeros((n_slots,), jnp.int32).at[order].set(dest)
    y = jnp.sum(yb[slot_row].reshape(n, TOP_K, D_MODEL) * gates[..., None], axis=1)
    return x + y


def _chunk_masks(tb):
    t = np.arange(tb)
    same = (t[:, None] // HG_CHUNK) == (t[None, :] // HG_CHUNK)
    tri = same & (t[None, :] <= t[:, None])
    upper = same & (t[None, :] > t[:, None])
    return jnp.asarray(tri, BF16), jnp.asarray(upper, BF16)


def kernel(x_prompt, x_sample, cache_k, cache_v, state_hgrn, page_table, w_in, hg_lower, hg_norm, q_norm, k_norm, w_branch_a, w_branch_b, w_out, attn_norm, ffn_norm, w_dense_gate, w_dense_up, w_dense_down, w_router, w_exp_gate, w_exp_up, w_exp_down):
    nb, seq, _ = x_prompt.shape
    db, dec_seq, _ = x_sample.shape
    depth, n_pool = cache_k.shape[:2]
    n_pages = page_table.shape[1]
    past_len = n_pages * PAGE_SIZE
    assert dec_seq == 1 and seq % MOBA_BLOCK == 0 and past_len % MOBA_BLOCK == 0
    n_rows_p = nb * seq
    assert n_rows_p % db == 0 and db % 8 == 0
    nblk = seq // MOBA_BLOCK
    n_blk_s = past_len // MOBA_BLOCK
    assert nblk >= MOBA_TOPK and n_blk_s >= MOBA_TOPK
    row_blk_s = n_rows_p // db

    slopes = 2.0 ** (-8.0 * jnp.arange(1, N_HEADS + 1, dtype=F32) / N_HEADS)
    lb_cum = jnp.cumsum(jax.nn.softmax(hg_lower.astype(F32), axis=0), axis=0)
    lower = lb_cum - lb_cum[:1]

    lane = np.arange(ATT_WIDTH)
    gmat = jnp.asarray((lane[:, None] // HEAD_DIM) == (lane[None, :] // HEAD_DIM), BF16)
    head_mat = jnp.asarray((lane[:, None] // HEAD_DIM) == np.arange(LANES)[None, :], BF16)
    tri, upper = _chunk_masks(MOBA_BLOCK)
    cache_k2 = cache_k.reshape(depth * n_pool, PAGE_SIZE, ATT_WIDTH)
    cache_v2 = cache_v.reshape(depth * n_pool, PAGE_SIZE, ATT_WIDTH)

    x = jnp.concatenate([x_prompt.reshape(n_rows_p, D_MODEL), x_sample.reshape(db, D_MODEL)], axis=0)
    outs = {name: [] for name in ("kp", "vp", "sp", "ks", "vs", "ss")}
    for l in range(depth):
        z = _proj(x, attn_norm[l], w_in[l].astype(BF16))
        lb = lower[l].reshape(1, HG_WIDTH)
        g_out = hg_norm[l].reshape(1, HG_HEAD_DIM)
        qn = jnp.tile(q_norm[l], N_HEADS).reshape(1, ATT_WIDTH)
        kn = jnp.tile(k_norm[l], N_HEADS).reshape(1, ATT_WIDTH)
        wa, wb, wo = (w[l].astype(BF16) for w in (w_branch_a, w_branch_b, w_out))

        oa_p, s_p = _hgrn_prompt(z, lb, g_out, tri, upper, nb, seq)
        k_p, v_p, qt, kh, vt, km = _qk_prompt(z, qn, kn, gmat, nb, nblk)
        kmh = km.reshape(nb, nblk, N_HEADS, HEAD_DIM).transpose(0, 2, 1, 3)
        ob_p = _moba_prompt(slopes, qt, kh, vt, kmh).reshape(nb, ATT_WIDTH, seq)

        oa_s, s_s = _hgrn_sample(z, lb, g_out, state_hgrn, l, row_blk_s, db)
        q_s, k_s, v_s = _qk_sample(z, qn, kn, gmat, row_blk_s, db)
        km_s = _cache_mean(page_table, cache_k2, l * n_pool, db, n_blk_s)
        sel = _sample_select(q_s, km_s, head_mat)[:, :MOBA_TOPK, :N_HEADS]
        sel_blk = sel.transpose(0, 2, 1)
        lpage = sel_blk[..., None] * PAGES_PER_BLOCK + jnp.arange(PAGES_PER_BLOCK)
        phys = jnp.take_along_axis(page_table, lpage.reshape(db, -1), axis=1)
        ob_s = _moba_sample(phys, sel_blk.reshape(db, -1), q_s, k_s, v_s, cache_k2, cache_v2,
                            l * n_pool, past_len)

        x = _merge(x, z, oa_p, ob_p, wa, wb, wo, tile=MOBA_BLOCK, row_blk0=0, n_tiles=nb * nblk,
                   ob_transposed=True, tiles_per_seq=nblk)
        x = _merge(x, z, oa_s, ob_s, wa, wb, wo, tile=db, row_blk0=row_blk_s, n_tiles=1,
                   ob_transposed=False)

        i = l // 2
        if l % 2 == 0:
            x = _ffn_dense(x, ffn_norm[l], w_dense_gate[i].astype(BF16), w_dense_up[i].astype(BF16),
                           w_dense_down[i].astype(BF16))
        else:
            x = _moe(x, ffn_norm[l], w_router[i], w_exp_gate[i].astype(BF16), w_exp_up[i].astype(BF16),
                     w_exp_down[i].astype(BF16))

        outs["kp"].append(k_p.reshape(nb, seq, N_HEADS, HEAD_DIM))
        outs["vp"].append(v_p.reshape(nb, seq, N_HEADS, HEAD_DIM))
        outs["sp"].append(s_p)
        outs["ks"].append(k_s.reshape(db, 1, N_HEADS, HEAD_DIM))
        outs["vs"].append(v_s.reshape(db, 1, N_HEADS, HEAD_DIM))
        outs["ss"].append(s_s)

    return (x[:n_rows_p].reshape(nb, seq, D_MODEL), x[n_rows_p:].reshape(db, 1, D_MODEL),
            jnp.stack(outs["kp"]), jnp.stack(outs["vp"]), jnp.stack(outs["sp"]),
            jnp.stack(outs["ks"]), jnp.stack(outs["vs"]), jnp.stack(outs["ss"]))
```

```python
import functools

import numpy as np
import jax
import jax.numpy as jnp
from jax import lax
from jax.experimental import pallas as pl
from jax.experimental.pallas import tpu as pltpu

F32 = jnp.float32
BF16 = jnp.bfloat16

D_MODEL = 1024
HG_WIDTH = 512
HG_HEAD_DIM = 128
HG_HEADS = HG_WIDTH // HG_HEAD_DIM
HG_CHUNK = 16
HEAD_DIM = 64
N_HEADS = 8
ATT_WIDTH = N_HEADS * HEAD_DIM
MOBA_BLOCK = 256
MOBA_TOPK = 3
PAGE_SIZE = 128
PAGES_PER_BLOCK = MOBA_BLOCK // PAGE_SIZE
D_FF = 2816
N_EXPERTS = 8
TOP_K = 2
RMS_EPS = 1e-6
IN_WIDTH = 4 * HG_WIDTH + 3 * ATT_WIDTH + 2 * D_MODEL
COL_ZQ, COL_ZF, COL_ZI, COL_ZG, COL_AQ, COL_AK, COL_AV = range(7)
COL_GA, COL_GB = 7, 9

KEY_WIDTH = 2 * HEAD_DIM
MAX_BLOCKS = 32
FEAT_OFF = MAX_BLOCKS + 3
FEAT_BLK = MAX_BLOCKS + 6
FEAT_END = MAX_BLOCKS + 12
LOG2E = 1.4426950408889634

MASKED = -1e30
VMEM_LIMIT = 56 * 1024 * 1024
FF_TILE = D_FF // 2
MOE_ROWS = 512
PAGE_GROUP = 16
LANES = 128


def _cparams(sem):
    return pltpu.CompilerParams(dimension_semantics=sem, vmem_limit_bytes=VMEM_LIMIT)


def _row_tile(n, hi):
    for t in range(hi, hi // 4, -16):
        if n % t == 0:
            return t
    return hi


def _split3(a):
    a1 = a.astype(BF16)
    r = a - a1.astype(F32)
    a2 = r.astype(BF16)
    a3 = (r - a2.astype(F32)).astype(BF16)
    return a1, a2, a3


def _dot(a, b):
    return jnp.dot(a, b, preferred_element_type=F32)


def _dot_nt(a, b):
    return lax.dot_general(a, b, (((1,), (1,)), ((), ())), preferred_element_type=F32)


def _dot_f32_lhs(a, b_exact):
    a1, a2, a3 = _split3(a)
    return _dot(a1, b_exact) + _dot(a2, b_exact) + _dot(a3, b_exact)


def _dot_f32_rhs(a_exact, b):
    b1, b2, b3 = _split3(b)
    return _dot(a_exact, b1) + _dot(a_exact, b2) + _dot(a_exact, b3)


def _rms(x, g):
    return x * lax.rsqrt(jnp.mean(x * x, axis=-1, keepdims=True) + RMS_EPS) * g


def _silu(x):
    return x / (1.0 + jnp.exp(-x))


def _sigmoid(x):
    return 1.0 / (1.0 + jnp.exp(-x))


def _top_mask_axis0(g, k):
    idx = lax.broadcasted_iota(jnp.int32, g.shape, 0)
    sel = jnp.zeros(g.shape, jnp.bool_)
    picks = []
    for _ in range(k):
        mx = jnp.max(g, axis=0, keepdims=True)
        first = jnp.min(jnp.where(g == mx, idx, g.shape[0]), axis=0, keepdims=True)
        hit = (idx == first) & (mx > -jnp.inf)
        sel = sel | hit
        g = jnp.where(idx == first, -jnp.inf, g)
        picks.append(first)
    return sel, picks


def _proj_kernel(x_ref, g_ref, w_ref, o_ref, h_scr):
    @pl.when(pl.program_id(1) == 0)
    def _():
        h_scr[...] = _rms(x_ref[...], g_ref[...]).astype(BF16)

    o_ref[...] = _dot(h_scr[...], w_ref[...])


def _proj(x, g, w_bf16):
    n = x.shape[0]
    tm = _row_tile(n, 1024)
    tn = IN_WIDTH // 4
    return pl.pallas_call(
        _proj_kernel,
        grid=(pl.cdiv(n, tm), IN_WIDTH // tn),
        in_specs=[pl.BlockSpec((tm, D_MODEL), lambda i, j: (i, 0)),
                  pl.BlockSpec((1, D_MODEL), lambda i, j: (0, 0)),
                  pl.BlockSpec((D_MODEL, tn), lambda i, j: (0, j))],
        out_specs=pl.BlockSpec((tm, tn), lambda i, j: (i, j)),
        out_shape=jax.ShapeDtypeStruct((n, IN_WIDTH), F32),
        scratch_shapes=[pltpu.VMEM((tm, D_MODEL), BF16)],
        compiler_params=_cparams(("parallel", "arbitrary")),
        name="in_proj",
    )(x, g.reshape(1, D_MODEL), w_bf16)


def _group_rms(x, gain, gmat):
    ms = _dot_f32_lhs(x * x, gmat) * (1.0 / HEAD_DIM)
    return x * lax.rsqrt(ms + RMS_EPS) * gain


def _qk_prompt_kernel(aq_ref, ak_ref, av_ref, qn_ref, kn_ref, gm_ref,
                      k_out, v_out, qt_out, kh_out, vt_out, km_out):
    j = pl.program_id(1)
    gmat = gm_ref[...]
    q = _group_rms(aq_ref[...], qn_ref[...], gmat)
    k = _group_rms(ak_ref[...], kn_ref[...], gmat)
    v = av_ref[...]
    qt_out[0, :, 0] = q.T.reshape(N_HEADS, HEAD_DIM, MOBA_BLOCK)
    vt_out[0, :, 0] = v.T.reshape(N_HEADS, HEAD_DIM, MOBA_BLOCK).astype(BF16)
    lane = lax.broadcasted_iota(jnp.int32, (MOBA_BLOCK, HEAD_DIM), 1)
    off = lax.broadcasted_iota(jnp.int32, (MOBA_BLOCK, HEAD_DIM), 0).astype(F32)
    feat = jnp.where(lane < FEAT_OFF, off,
                     jnp.where(lane < FEAT_BLK, j.astype(F32), jnp.where(lane < FEAT_END, 1.0, 0.0)))
    ext = jnp.where(lane < MAX_BLOCKS, (lane == j).astype(F32), feat).astype(BF16)
    kb = k.astype(BF16)
    for h in range(N_HEADS):
        cols = slice(h * HEAD_DIM, (h + 1) * HEAD_DIM)
        k_out[:, h, :] = k[:, cols]
        v_out[:, h, :] = v[:, cols]
        kh_out[0, h, 0] = jnp.concatenate([kb[:, cols], ext], axis=1)
    km_out[0, 0] = jnp.sum(k, axis=0, keepdims=True) * (1.0 / MOBA_BLOCK)


def _qk_prompt(z, qn, kn, gmat, nb, nblk):
    rows = nb * nblk * MOBA_BLOCK
    zspec = lambda c: pl.BlockSpec((MOBA_BLOCK, ATT_WIDTH), lambda b, i, c=c: (b * nblk + i, c))
    vec = pl.BlockSpec((1, ATT_WIDTH), lambda b, i: (0, 0))
    head5 = lambda r, c: pl.BlockSpec((1, N_HEADS, 1, r, c), lambda b, i: (b, 0, i, 0, 0))
    return pl.pallas_call(
        _qk_prompt_kernel,
        grid=(nb, nblk),
        in_specs=[zspec(COL_AQ), zspec(COL_AK), zspec(COL_AV), vec, vec,
                  pl.BlockSpec((ATT_WIDTH, ATT_WIDTH), lambda b, i: (0, 0))],
        out_specs=[pl.BlockSpec((MOBA_BLOCK, N_HEADS, HEAD_DIM), lambda b, i: (b * nblk + i, 0, 0)),
                   pl.BlockSpec((MOBA_BLOCK, N_HEADS, HEAD_DIM), lambda b, i: (b * nblk + i, 0, 0)),
                   head5(HEAD_DIM, MOBA_BLOCK), head5(MOBA_BLOCK, KEY_WIDTH), head5(HEAD_DIM, MOBA_BLOCK),
                   pl.BlockSpec((1, 1, 1, ATT_WIDTH), lambda b, i: (b, i, 0, 0))],
        out_shape=[jax.ShapeDtypeStruct((rows, N_HEADS, HEAD_DIM), F32),
                   jax.ShapeDtypeStruct((rows, N_HEADS, HEAD_DIM), F32),
                   jax.ShapeDtypeStruct((nb, N_HEADS, nblk, HEAD_DIM, MOBA_BLOCK), F32),
                   jax.ShapeDtypeStruct((nb, N_HEADS, nblk, MOBA_BLOCK, KEY_WIDTH), BF16),
                   jax.ShapeDtypeStruct((nb, N_HEADS, nblk, HEAD_DIM, MOBA_BLOCK), BF16),
                   jax.ShapeDtypeStruct((nb, nblk, 1, ATT_WIDTH), F32)],
        compiler_params=_cparams(("parallel", "parallel")),
        name="qk_prompt",
    )(z, z, z, qn, kn, gmat)


def _qk_sample_kernel(aq_ref, ak_ref, av_ref, qn_ref, kn_ref, gm_ref, q_out, k_out, v_out):
    gmat = gm_ref[...]
    q_out[...] = _group_rms(aq_ref[...], qn_ref[...], gmat)
    k_out[...] = _group_rms(ak_ref[...], kn_ref[...], gmat)
    v_out[...] = av_ref[...]


def _qk_sample(z, qn, kn, gmat, row_blk, db):
    zspec = lambda c: pl.BlockSpec((db, ATT_WIDTH), lambda i, c=c: (row_blk, c))
    vec = pl.BlockSpec((1, ATT_WIDTH), lambda i: (0, 0))
    out = pl.BlockSpec((db, ATT_WIDTH), lambda i: (0, 0))
    shp = jax.ShapeDtypeStruct((db, ATT_WIDTH), F32)
    return pl.pallas_call(
        _qk_sample_kernel,
        grid=(1,),
        in_specs=[zspec(COL_AQ), zspec(COL_AK), zspec(COL_AV), vec, vec,
                  pl.BlockSpec((ATT_WIDTH, ATT_WIDTH), lambda i: (0, 0))],
        out_specs=[out, out, out],
        out_shape=[shp, shp, shp],
        compiler_params=_cparams(("arbitrary",)),
        name="qk_sample",
    )(z, z, z, qn, kn, gmat)


def _moba_prompt_kernel(slopes_ref, qt_ref, kh_ref, vt_ref, km_ref, o_ref,
                        qa_scr, s_scr, p_scr, m_scr, l_scr, acc_scr):
    i = pl.program_id(1)
    heads = range(N_HEADS)
    n_feat_rows = KEY_WIDTH - HEAD_DIM - MAX_BLOCKS
    qry_pos = lax.broadcasted_iota(jnp.int32, (1, MOBA_BLOCK), 1).astype(F32)
    feat_row = lax.broadcasted_iota(jnp.int32, (n_feat_rows, MOBA_BLOCK), 0)
    blk = lax.broadcasted_iota(jnp.int32, (MAX_BLOCKS, MOBA_BLOCK), 0)
    causal = (lax.broadcasted_iota(jnp.int32, (MOBA_BLOCK, MOBA_BLOCK), 0)
              <= lax.broadcasted_iota(jnp.int32, (MOBA_BLOCK, MOBA_BLOCK), 1))
    ones = jnp.ones((1, MOBA_BLOCK), F32)

    gates = [_dot(km_ref[0, h].astype(BF16), qt_ref[0, h, 0].astype(BF16)) for h in heads]
    for h in heads:
        qt = qt_ref[0, h, 0]
        sel, _ = _top_mask_axis0(jnp.where(blk < i, gates[h], -jnp.inf), MOBA_TOPK)
        sel_bias = jnp.where(sel | (blk >= i), 0.0, MASKED)
        c = slopes_ref[h] * LOG2E
        coefs = (c * ones, (c * MOBA_BLOCK) * ones, -c * qry_pos,
                 (-(c * MOBA_BLOCK) * i.astype(F32)) * ones)
        feat = jnp.zeros((n_feat_rows, MOBA_BLOCK), F32)
        for r, term in enumerate(t for v in coefs for t in _split3(v)):
            feat = jnp.where(feat_row == r, term.astype(F32), feat)
        q_aug = jnp.concatenate([(qt * (HEAD_DIM ** -0.5 * LOG2E)).astype(BF16), sel_bias.astype(BF16),
                                 feat.astype(BF16)], axis=0)
        qa_scr[h] = q_aug

    for h in heads:
        s_scr[h] = jnp.where(causal, _dot(kh_ref[0, h, i], qa_scr[h]), MASKED)
    for h in heads:
        s = s_scr[h]
        m = jnp.max(s, axis=0, keepdims=True)
        p = jnp.exp2(s - m)
        p_scr[h] = p.astype(BF16)
        m_scr[h] = m
        l_scr[h] = jnp.sum(p, axis=0, keepdims=True)
    for h in heads:
        acc_scr[h] = _dot(vt_ref[0, h, i], p_scr[h])

    def body(j, carry):
        for h in heads:
            s_scr[h] = _dot(kh_ref[0, h, j], qa_scr[h])
        m_old = [m_scr[h] for h in heads]
        l_old = [l_scr[h] for h in heads]
        m_new, l_new, alpha = [], [], []
        for h in heads:
            s = s_scr[h]
            m_new.append(jnp.maximum(m_old[h], jnp.max(s, axis=0, keepdims=True)))
            alpha.append(jnp.exp2(m_old[h] - m_new[h]))
            p = jnp.exp2(s - m_new[h])
            p_scr[h] = p.astype(BF16)
            l_new.append(alpha[h] * l_old[h] + jnp.sum(p, axis=0, keepdims=True))
        for h in heads:
            m_scr[h] = m_new[h]
            l_scr[h] = l_new[h]
        acc_old = [acc_scr[h] for h in heads]
        pv = [_dot(vt_ref[0, h, j], p_scr[h]) for h in heads]
        for h in heads:
            acc_scr[h] = alpha[h] * acc_old[h] + pv[h]
        return carry

    lax.fori_loop(0, i, body, 0)
    for h in heads:
        o_ref[0, h] = acc_scr[h] / l_scr[h]


def _moba_prompt(slopes, qt, kh, vt, kmh):
    nb, _, nblk = qt.shape[:3]
    resident = lambda shape: pl.BlockSpec(shape, lambda b, i, s: (b, 0, 0, 0, 0), pipeline_mode=pl.Buffered(1))
    return pl.pallas_call(
        _moba_prompt_kernel,
        grid_spec=pltpu.PrefetchScalarGridSpec(
            num_scalar_prefetch=1,
            grid=(nb, nblk),
            in_specs=[pl.BlockSpec((1, N_HEADS, 1, HEAD_DIM, MOBA_BLOCK), lambda b, i, s: (b, 0, i, 0, 0)),
                      resident((1, N_HEADS, nblk, MOBA_BLOCK, KEY_WIDTH)),
                      resident((1, N_HEADS, nblk, HEAD_DIM, MOBA_BLOCK)),
                      pl.BlockSpec((1, N_HEADS, MAX_BLOCKS, HEAD_DIM), lambda b, i, s: (b, 0, 0, 0))],
            out_specs=pl.BlockSpec((1, N_HEADS, HEAD_DIM, MOBA_BLOCK), lambda b, i, s: (b, 0, 0, i)),
            scratch_shapes=[pltpu.VMEM((N_HEADS, KEY_WIDTH, MOBA_BLOCK), BF16),
                            pltpu.VMEM((N_HEADS, MOBA_BLOCK, MOBA_BLOCK), F32),
                            pltpu.VMEM((N_HEADS, MOBA_BLOCK, MOBA_BLOCK), BF16),
                            pltpu.VMEM((N_HEADS, 1, MOBA_BLOCK), F32),
                            pltpu.VMEM((N_HEADS, 1, MOBA_BLOCK), F32),
                            pltpu.VMEM((N_HEADS, HEAD_DIM, MOBA_BLOCK), F32)]),
        out_shape=jax.ShapeDtypeStruct((nb, N_HEADS, HEAD_DIM, nblk * MOBA_BLOCK), F32),
        compiler_params=_cparams(("parallel", "arbitrary")),
        name="moba_prompt",
    )(slopes, qt, kh, vt, kmh)


def _hgrn_gates(zf, lb):
    log_sig = jnp.minimum(zf, 0.0) - jnp.log(1.0 + jnp.exp(-jnp.abs(zf)))
    a = jnp.log(lb)
    b = jnp.log(1.0 - lb) + log_sig
    logf = jnp.maximum(a, b) + jnp.log(1.0 + jnp.exp(-jnp.abs(a - b)))
    k = (1.0 - lb) / (1.0 + jnp.exp(zf))
    return logf, k


def _hgrn_prompt_kernel(zq_ref, zf_ref, zi_ref, zg_ref, lb_ref, go_ref, tri_ref, upper_ref,
                        o_ref, s_ref, st_scr, oi_scr):
    i = pl.program_id(1)
    tb = zq_ref.shape[0]

    @pl.when(i == 0)
    def _():
        st_scr[...] = jnp.zeros_like(st_scr)

    tri = tri_ref[...]
    upper = upper_ref[...]
    chunk = lax.broadcasted_iota(jnp.int32, (tb, HG_HEAD_DIM), 0) // HG_CHUNK
    head_cols = [slice(h * HG_HEAD_DIM, (h + 1) * HG_HEAD_DIM) for h in range(HG_HEADS)]

    heads = []
    for cols in head_cols:
        v = zi_ref[:, cols]
        logf, k = _hgrn_gates(zf_ref[:, cols], lb_ref[:, cols])
        b = _dot_f32_rhs(tri, logf)
        rest = _dot_f32_rhs(upper, logf)
        q_in = (zq_ref[:, cols] * jnp.exp(b)).astype(BF16)
        k_in = (k * jnp.exp(-b)).astype(BF16)
        k_out = k * jnp.exp(rest)
        dec = jnp.exp(b + rest)
        att = jnp.where(tri > 0, _dot_nt(q_in, k_in), 0.0).astype(BF16)
        o_intra = _dot(att, v.astype(BF16))
        heads.append((q_in, k_out, dec, v.T.astype(BF16), o_intra))

    sts = [st_scr[h] for h in range(HG_HEADS)]
    for n in range(tb // HG_CHUNK):
        rows = slice(n * HG_CHUNK, (n + 1) * HG_CHUNK)
        for h, cols in enumerate(head_cols):
            q_in, k_out, dec, vt, _ = heads[h]
            oi_scr[rows, cols] = _dot_nt(q_in[rows], sts[h].astype(BF16))
            k_n = jnp.where(chunk == n, k_out, 0.0).astype(BF16)
            sts[h] = sts[h] * dec[n * HG_CHUNK:n * HG_CHUNK + 1, :] + _dot(vt, k_n)

    go = go_ref[...]
    for h, cols in enumerate(head_cols):
        st_scr[h] = sts[h]
        o = heads[h][4] + oi_scr[:, cols]
        o_ref[:, cols] = _rms(o, go) * _silu(zg_ref[:, cols])

    @pl.when(i == pl.num_programs(1) - 1)
    def _():
        for h in range(HG_HEADS):
            s_ref[0, h] = sts[h].T


def _hgrn_prompt(z, lb, g_out, tri, upper, nb, seq):
    tb = MOBA_BLOCK
    nt = seq // tb
    zspec = lambda c: pl.BlockSpec((tb, HG_WIDTH), lambda b, i, c=c: (b * nt + i, c))
    sq = pl.BlockSpec((tb, tb), lambda b, i: (0, 0))
    return pl.pallas_call(
        _hgrn_prompt_kernel,
        grid=(nb, nt),
        in_specs=[zspec(COL_ZQ), zspec(COL_ZF), zspec(COL_ZI), zspec(COL_ZG),
                  pl.BlockSpec((1, HG_WIDTH), lambda b, i: (0, 0)),
                  pl.BlockSpec((1, HG_HEAD_DIM), lambda b, i: (0, 0)),
                  sq, sq],
        out_specs=[pl.BlockSpec((tb, HG_WIDTH), lambda b, i: (b * nt + i, 0)),
                   pl.BlockSpec((1, HG_HEADS, HG_HEAD_DIM, HG_HEAD_DIM), lambda b, i: (b, 0, 0, 0))],
        out_shape=[jax.ShapeDtypeStruct((nb * seq, HG_WIDTH), F32),
                   jax.ShapeDtypeStruct((nb, HG_HEADS, HG_HEAD_DIM, HG_HEAD_DIM), F32)],
        scratch_shapes=[pltpu.VMEM((HG_HEADS, HG_HEAD_DIM, HG_HEAD_DIM), F32),
                        pltpu.VMEM((tb, HG_WIDTH), F32)],
        compiler_params=_cparams(("parallel", "arbitrary")),
        name="hgrn_prompt",
    )(z, z, z, z, lb, g_out, tri, upper)


def _hgrn_sample_kernel(zq_ref, zf_ref, zi_ref, zg_ref, lb_ref, go_ref, s_ref, o_ref, sn_ref):
    db = zq_ref.shape[0]
    eye = (lax.broadcasted_iota(jnp.int32, (HG_HEAD_DIM, HG_HEAD_DIM), 0)
           == lax.broadcasted_iota(jnp.int32, (HG_HEAD_DIM, HG_HEAD_DIM), 1))
    col = lambda r: jnp.sum(jnp.where(eye, r, 0.0), axis=1, keepdims=True)
    lb = lb_ref[...]
    go = go_ref[...]

    def body(b, carry):
        row = pl.ds(b, 1)
        logf, k = _hgrn_gates(zf_ref[row, :], lb)
        s_new = s_ref[0, b, 0] * col(jnp.exp(logf)) + col(k) * zi_ref[row, :]
        sn_ref[b, 0] = s_new
        o = jnp.sum(s_new * col(zq_ref[row, :]), axis=0, keepdims=True)
        o_ref[row, :] = _rms(o, go) * _silu(zg_ref[row, :])
        return carry

    lax.fori_loop(0, db, body, 0)


def _hgrn_sample(z, lb, g_out, state, layer, row_blk, db):
    zspec = lambda c: pl.BlockSpec((db, HG_HEAD_DIM), lambda h, c=c: (row_blk, 4 * c + h))
    return pl.pallas_call(
        _hgrn_sample_kernel,
        grid=(HG_HEADS,),
        in_specs=[zspec(COL_ZQ), zspec(COL_ZF), zspec(COL_ZI), zspec(COL_ZG),
                  pl.BlockSpec((1, HG_HEAD_DIM), lambda h: (0, h)),
                  pl.BlockSpec((1, HG_HEAD_DIM), lambda h: (0, 0)),
                  pl.BlockSpec((1, db, 1, HG_HEAD_DIM, HG_HEAD_DIM), lambda h: (layer, 0, h, 0, 0))],
        out_specs=[pl.BlockSpec((db, HG_HEAD_DIM), lambda h: (0, h)),
                   pl.BlockSpec((db, 1, HG_HEAD_DIM, HG_HEAD_DIM), lambda h: (0, h, 0, 0))],
        out_shape=[jax.ShapeDtypeStruct((db, HG_WIDTH), F32),
                   jax.ShapeDtypeStruct((db, HG_HEADS, HG_HEAD_DIM, HG_HEAD_DIM), F32)],
        compiler_params=_cparams(("parallel",)),
        name="hgrn_sample",
    )(z, z, z, z, lb, g_out, state)


def _cache_mean_kernel(pt_ref, *refs):
    pages, o_ref = refs[:-1], refs[-1]
    g = pl.program_id(1)
    per_step = len(pages) // PAGES_PER_BLOCK
    for c in range(per_step):
        tot = jnp.sum(pages[PAGES_PER_BLOCK * c][0, 0], axis=0)
        for p in range(1, PAGES_PER_BLOCK):
            tot = tot + jnp.sum(pages[PAGES_PER_BLOCK * c + p][0, 0], axis=0)
        o_ref[0, g * per_step + c] = tot * (1.0 / MOBA_BLOCK)


def _cache_mean(page_table, cache, layer, db, n_blk):
    n_pages = n_blk * PAGES_PER_BLOCK
    pg = PAGE_GROUP if n_pages % PAGE_GROUP == 0 else PAGES_PER_BLOCK
    spec = lambda k: pl.BlockSpec((1, 1, PAGE_SIZE, N_HEADS, HEAD_DIM),
                                  lambda b, g, pt, k=k: (layer, pt[b, g * pg + k], 0, 0, 0))
    return pl.pallas_call(
        _cache_mean_kernel,
        grid_spec=pltpu.PrefetchScalarGridSpec(
            num_scalar_prefetch=1,
            grid=(db, n_pages // pg),
            in_specs=[spec(k) for k in range(pg)],
            out_specs=pl.BlockSpec((1, n_blk, N_HEADS, HEAD_DIM), lambda b, g, pt: (b, 0, 0, 0))),
        out_shape=jax.ShapeDtypeStruct((db, n_blk, N_HEADS, HEAD_DIM), F32),
        compiler_params=_cparams(("parallel", "arbitrary")),
        name="cache_block_mean",
    )(page_table, *([cache] * pg))


def _sample_select_kernel(q_ref, km_ref, o_ref):
    b = pl.program_id(0)
    km = km_ref[0].astype(BF16).astype(F32)
    q = q_ref[pl.ds(b, 1)].astype(BF16).astype(F32)
    gate = jnp.sum(km * q, axis=-1, keepdims=True)
    _, picks = _top_mask_axis0(gate, MOBA_TOPK)
    for t, pick in enumerate(picks):
        o_ref[0, t] = jnp.broadcast_to(pick[0], (N_HEADS, LANES))


def _sample_select(q_s3, km_s):
    db, n_blk = km_s.shape[:2]
    return pl.pallas_call(
        _sample_select_kernel,
        grid=(db,),
        in_specs=[pl.BlockSpec((db, N_HEADS, HEAD_DIM), lambda b: (0, 0, 0)),
                  pl.BlockSpec((1, n_blk, N_HEADS, HEAD_DIM), lambda b: (b, 0, 0, 0))],
        out_specs=pl.BlockSpec((1, MOBA_TOPK, N_HEADS, LANES), lambda b: (b, 0, 0, 0)),
        out_shape=jax.ShapeDtypeStruct((db, MOBA_TOPK, N_HEADS, LANES), jnp.int32),
        compiler_params=_cparams(("parallel",)),
        name="sample_select",
    )(q_s3, km_s)


def _moba_sample_kernel(layer, past_len, phys_ref, blk_ref, q_ref, k_ref, v_ref, ck_ref, cv_ref, o_ref,
                        kbuf, vbuf, sem):
    pages_per_head = MOBA_TOPK * PAGES_PER_BLOCK
    b = pl.program_id(0)

    def slab_copies(h, t):
        page = phys_ref[b, h * pages_per_head + t]
        dst = pl.ds(t * PAGE_SIZE, PAGE_SIZE)
        idx = h * pages_per_head + t
        return (pltpu.make_async_copy(ck_ref.at[layer, page, :, h, :], kbuf.at[h, dst, :], sem.at[0, idx]),
                pltpu.make_async_copy(cv_ref.at[layer, page, :, h, :], vbuf.at[h, dst, :], sem.at[1, idx]))

    for h in range(N_HEADS):
        for t in range(pages_per_head):
            for cp in slab_copies(h, t):
                cp.start()

    row = pl.ds(b, 1)
    m_keys = MOBA_TOPK * MOBA_BLOCK
    lane = lax.broadcasted_iota(jnp.int32, (1, m_keys), 1)
    q_row, k_row, v_row = q_ref[row, :], k_ref[row, :], v_ref[row, :]
    outs = []
    for h in range(N_HEADS):
        cols = slice(h * HEAD_DIM, (h + 1) * HEAD_DIM)
        for t in range(pages_per_head):
            for cp in slab_copies(h, t):
                cp.wait()
        kc = kbuf[h].astype(BF16)
        vc = vbuf[h].astype(BF16)
        q = q_row[:, cols] * (HEAD_DIM ** -0.5)
        s = _dot_nt(jnp.broadcast_to(q, (8, HEAD_DIM)).astype(BF16), kc)[0:1]
        kblk = blk_ref[b, h * MOBA_TOPK + MOBA_TOPK - 1]
        for t in range(MOBA_TOPK - 2, -1, -1):
            kblk = jnp.where(lane < (t + 1) * MOBA_BLOCK, blk_ref[b, h * MOBA_TOPK + t], kblk)
        kpos = kblk * MOBA_BLOCK + lane % MOBA_BLOCK
        slope = 2.0 ** (-8.0 * (h + 1) / N_HEADS)
        s = s - slope * (past_len - kpos).astype(F32)
        s_self = jnp.sum(q * k_row[:, cols], axis=1, keepdims=True)
        m = jnp.maximum(jnp.max(s, axis=1, keepdims=True), s_self)
        p = jnp.exp(s - m)
        p_self = jnp.exp(s_self - m)
        l = jnp.sum(p, axis=1, keepdims=True) + p_self
        o = _dot(jnp.broadcast_to(p, (8, m_keys)).astype(BF16), vc)[0:1] + p_self * v_row[:, cols]
        outs.append(o / l)
    o_ref[0] = jnp.concatenate(outs, axis=1)


def _moba_sample(phys, sel_blk, q_s, k_s, v_s, cache_k, cache_v, layer, past_len):
    db = q_s.shape[0]
    n_sel_pages = N_HEADS * MOBA_TOPK * PAGES_PER_BLOCK
    m_keys = MOBA_TOPK * MOBA_BLOCK
    full = pl.BlockSpec((db, ATT_WIDTH), lambda b, ph, sb: (0, 0))
    hbm = pl.BlockSpec(memory_space=pl.ANY)
    out = pl.pallas_call(
        functools.partial(_moba_sample_kernel, layer, past_len),
        grid_spec=pltpu.PrefetchScalarGridSpec(
            num_scalar_prefetch=2,
            grid=(db,),
            in_specs=[full, full, full, hbm, hbm],
            out_specs=pl.BlockSpec((1, 1, ATT_WIDTH), lambda b, ph, sb: (b, 0, 0)),
            scratch_shapes=[pltpu.VMEM((N_HEADS, m_keys, HEAD_DIM), F32),
                            pltpu.VMEM((N_HEADS, m_keys, HEAD_DIM), F32),
                            pltpu.SemaphoreType.DMA((2, n_sel_pages))]),
        out_shape=jax.ShapeDtypeStruct((db, 1, ATT_WIDTH), F32),
        compiler_params=_cparams(("arbitrary",)),
        name="moba_sample",
    )(phys, sel_blk, q_s, k_s, v_s, cache_k, cache_v)
    return out.reshape(db, ATT_WIDTH)


def _merge_kernel(ob_transposed, x_ref, oa_ref, ob_ref, ga0, ga1, gb0, gb1, wa_ref, wb_ref, wo_ref, o_ref):
    oa = oa_ref[...].astype(BF16)
    ob = ob_ref[0].T if ob_transposed else ob_ref[...]
    a = _dot(oa, wa_ref[...])
    bm = _dot(ob.astype(BF16), wb_ref[...])
    half = D_MODEL // 2
    m0 = _sigmoid(ga0[...]) * a[:, :half] + _sigmoid(gb0[...]) * bm[:, :half]
    m1 = _sigmoid(ga1[...]) * a[:, half:] + _sigmoid(gb1[...]) * bm[:, half:]
    y = _dot(m0.astype(BF16), wo_ref[:half, :]) + _dot(m1.astype(BF16), wo_ref[half:, :])
    o_ref[...] = x_ref[...] + y


def _merge(x, z, o_a, o_b, wa, wb, wo, *, tile, row_blk0, n_tiles, ob_transposed, tiles_per_seq=None):
    rspec = lambda w, c: pl.BlockSpec((tile, w), lambda i, c=c: (row_blk0 + i, c))
    if ob_transposed:
        ob_spec = pl.BlockSpec((1, ATT_WIDTH, tile), lambda i: (i // tiles_per_seq, 0, i % tiles_per_seq))
    else:
        ob_spec = pl.BlockSpec((tile, ATT_WIDTH), lambda i: (i, 0))
    wspec = lambda r: pl.BlockSpec((r, D_MODEL), lambda i: (0, 0))
    return pl.pallas_call(
        functools.partial(_merge_kernel, ob_transposed),
        grid=(n_tiles,),
        in_specs=[rspec(D_MODEL, 0),
                  pl.BlockSpec((tile, HG_WIDTH), lambda i: (i, 0)), ob_spec,
                  rspec(512, COL_GA), rspec(512, COL_GA + 1), rspec(512, COL_GB), rspec(512, COL_GB + 1),
                  wspec(HG_WIDTH), wspec(ATT_WIDTH), wspec(D_MODEL)],
        out_specs=rspec(D_MODEL, 0),
        out_shape=jax.ShapeDtypeStruct(x.shape, F32),
        input_output_aliases={0: 0},
        compiler_params=_cparams(("parallel",)),
        name="merge_t" if ob_transposed else "merge",
    )(x, o_a, o_b, z, z, z, z, wa, wb, wo)


def _ffn_kernel(x_ref, g_ref, wg_ref, wu_ref, wd_ref, o_ref, h_scr, acc_scr):
    j = pl.program_id(1)

    @pl.when(j == 0)
    def _():
        h_scr[...] = _rms(x_ref[...], g_ref[...]).astype(BF16)
        acc_scr[...] = jnp.zeros_like(acc_scr)

    h = h_scr[...]
    a = _silu(_dot(h, wg_ref[...])) * _dot(h, wu_ref[...])
    acc_scr[...] += _dot(a.astype(BF16), wd_ref[...])

    @pl.when(j == pl.num_programs(1) - 1)
    def _():
        o_ref[...] = x_ref[...] + acc_scr[...]


def _ffn_dense(x, g, wg, wu, wd):
    n = x.shape[0]
    tm = _row_tile(n, 512)
    return pl.pallas_call(
        _ffn_kernel,
        grid=(pl.cdiv(n, tm), D_FF // FF_TILE),
        in_specs=[pl.BlockSpec((tm, D_MODEL), lambda i, j: (i, 0)),
                  pl.BlockSpec((1, D_MODEL), lambda i, j: (0, 0)),
                  pl.BlockSpec((D_MODEL, FF_TILE), lambda i, j: (0, j)),
                  pl.BlockSpec((D_MODEL, FF_TILE), lambda i, j: (0, j)),
                  pl.BlockSpec((FF_TILE, D_MODEL), lambda i, j: (j, 0))],
        out_specs=pl.BlockSpec((tm, D_MODEL), lambda i, j: (i, 0)),
        out_shape=jax.ShapeDtypeStruct((n, D_MODEL), F32),
        scratch_shapes=[pltpu.VMEM((tm, D_MODEL), BF16), pltpu.VMEM((tm, D_MODEL), F32)],
        compiler_params=_cparams(("parallel", "arbitrary")),
        name="ffn_dense",
    )(x, g.reshape(1, D_MODEL), wg, wu, wd)


def _router_kernel(x_ref, g_ref, wr_ref, h_ref, logit_ref):
    h = _rms(x_ref[...], g_ref[...]).astype(BF16)
    h_ref[...] = h
    logit_ref[...] = _dot(h, wr_ref[...].astype(BF16))


def _router(x, g, wr_pad):
    n = x.shape[0]
    tm = _row_tile(n, 1024)
    return pl.pallas_call(
        _router_kernel,
        grid=(pl.cdiv(n, tm),),
        in_specs=[pl.BlockSpec((tm, D_MODEL), lambda i: (i, 0)),
                  pl.BlockSpec((1, D_MODEL), lambda i: (0, 0)),
                  pl.BlockSpec((D_MODEL, LANES), lambda i: (0, 0))],
        out_specs=[pl.BlockSpec((tm, D_MODEL), lambda i: (i, 0)),
                   pl.BlockSpec((tm, LANES), lambda i: (i, 0))],
        out_shape=[jax.ShapeDtypeStruct((n, D_MODEL), BF16),
                   jax.ShapeDtypeStruct((n, LANES), F32)],
        compiler_params=_cparams(("parallel",)),
        name="moe_router",
    )(x, g.reshape(1, D_MODEL), wr_pad)


def _expert_kernel(be_ref, used_ref, x_ref, wg_ref, wu_ref, wd_ref, o_ref, acc_scr):
    i = pl.program_id(0)
    j = pl.program_id(1)

    @pl.when(j == 0)
    def _():
        acc_scr[...] = jnp.zeros_like(acc_scr)

    @pl.when(used_ref[i] > 0)
    def _():
        x = x_ref[...]
        a = _silu(_dot(x, wg_ref[0])) * _dot(x, wu_ref[0])
        acc_scr[...] += _dot(a.astype(BF16), wd_ref[0])

    @pl.when(j == pl.num_programs(1) - 1)
    def _():
        o_ref[...] = acc_scr[...]


def _experts(blk_e, blk_used, xb, weg, weu, wed):
    rows = xb.shape[0]
    return pl.pallas_call(
        _expert_kernel,
        grid_spec=pltpu.PrefetchScalarGridSpec(
            num_scalar_prefetch=2,
            grid=(rows // MOE_ROWS, D_FF // FF_TILE),
            in_specs=[pl.BlockSpec((MOE_ROWS, D_MODEL), lambda i, j, be, bu: (i, 0)),
                      pl.BlockSpec((1, D_MODEL, FF_TILE), lambda i, j, be, bu: (be[i], 0, j)),
                      pl.BlockSpec((1, D_MODEL, FF_TILE), lambda i, j, be, bu: (be[i], 0, j)),
                      pl.BlockSpec((1, FF_TILE, D_MODEL), lambda i, j, be, bu: (be[i], j, 0))],
            out_specs=pl.BlockSpec((MOE_ROWS, D_MODEL), lambda i, j, be, bu: (i, 0)),
            scratch_shapes=[pltpu.VMEM((MOE_ROWS, D_MODEL), F32)]),
        out_shape=jax.ShapeDtypeStruct((rows, D_MODEL), F32),
        compiler_params=_cparams(("parallel", "arbitrary")),
        name="moe_experts",
    )(blk_e, blk_used, xb, weg, weu, wed)


def _moe(x, g, wr, weg, weu, wed):
    n = x.shape[0]
    wr_pad = jnp.zeros((D_MODEL, LANES), F32).at[:, :N_EXPERTS].set(wr)
    h, logits = _router(x, g, wr_pad)
    top_l, top_e = lax.top_k(logits[:, :N_EXPERTS], TOP_K)
    gates = jax.nn.softmax(top_l, axis=-1)
    n_slots = n * TOP_K
    n_blocks = -(-(n_slots + N_EXPERTS * (MOE_ROWS - 1)) // MOE_ROWS)
    flat_e = top_e.reshape(-1)
    order = jnp.argsort(flat_e)
    sorted_e = flat_e[order]
    counts = jnp.bincount(flat_e, length=N_EXPERTS)
    padded = (counts + MOE_ROWS - 1) // MOE_ROWS * MOE_ROWS
    start_sorted = jnp.cumsum(counts) - counts
    end_pad = jnp.cumsum(padded)
    start_pad = end_pad - padded
    dest = (start_pad[sorted_e] + jnp.arange(n_slots) - start_sorted[sorted_e]).astype(jnp.int32)
    row_tok = jnp.full((n_blocks * MOE_ROWS,), n, jnp.int32).at[dest].set((order // TOP_K).astype(jnp.int32))
    blk_start = jnp.arange(n_blocks) * MOE_ROWS
    blk_e = jnp.minimum(jnp.searchsorted(end_pad, blk_start, side='right'), N_EXPERTS - 1).astype(jnp.int32)
    blk_used = (blk_start < end_pad[-1]).astype(jnp.int32)
    xb = jnp.concatenate([h, jnp.zeros((1, D_MODEL), BF16)], axis=0)[row_tok]
    yb = _experts(blk_e, blk_used, xb, weg, weu, wed)
    slot_row = jnp.zeros((n_slots,), jnp.int32).at[order].set(dest)
    y = jnp.sum(yb[slot_row].reshape(n, TOP_K, D_MODEL) * gates[..., None], axis=1)
    return x + y


def _chunk_masks(tb):
    t = np.arange(tb)
    same = (t[:, None] // HG_CHUNK) == (t[None, :] // HG_CHUNK)
    tri = same & (t[None, :] <= t[:, None])
    upper = same & (t[None, :] > t[:, None])
    return jnp.asarray(tri, BF16), jnp.asarray(upper, BF16)


def kernel(x_prompt, x_sample, cache_k, cache_v, state_hgrn, page_table, w_in, hg_lower, hg_norm, q_norm, k_norm, w_branch_a, w_branch_b, w_out, attn_norm, ffn_norm, w_dense_gate, w_dense_up, w_dense_down, w_router, w_exp_gate, w_exp_up, w_exp_down):
    nb, seq, _ = x_prompt.shape
    db, dec_seq, _ = x_sample.shape
    depth, n_pool = cache_k.shape[:2]
    n_pages = page_table.shape[1]
    past_len = n_pages * PAGE_SIZE
    assert dec_seq == 1 and seq % MOBA_BLOCK == 0 and past_len % MOBA_BLOCK == 0
    n_rows_p = nb * seq
    assert n_rows_p % db == 0 and db % 8 == 0
    nblk = seq // MOBA_BLOCK
    n_blk_s = past_len // MOBA_BLOCK
    assert MOBA_TOPK <= nblk <= MAX_BLOCKS and n_blk_s >= MOBA_TOPK
    row_blk_s = n_rows_p // db

    slopes = 2.0 ** (-8.0 * jnp.arange(1, N_HEADS + 1, dtype=F32) / N_HEADS)
    lb_cum = jnp.cumsum(jax.nn.softmax(hg_lower.astype(F32), axis=0), axis=0)
    lower = lb_cum - lb_cum[:1]

    lane = np.arange(ATT_WIDTH)
    gmat = jnp.asarray((lane[:, None] // HEAD_DIM) == (lane[None, :] // HEAD_DIM), BF16)
    tri, upper = _chunk_masks(MOBA_BLOCK)

    x = jnp.concatenate([x_prompt.reshape(n_rows_p, D_MODEL), x_sample.reshape(db, D_MODEL)], axis=0)
    outs = {name: [] for name in ("kp", "vp", "sp", "ks", "vs", "ss")}
    for l in range(depth):
        z = _proj(x, attn_norm[l], w_in[l].astype(BF16))
        lb = lower[l].reshape(1, HG_WIDTH)
        g_out = hg_norm[l].reshape(1, HG_HEAD_DIM)
        qn = jnp.tile(q_norm[l], N_HEADS).reshape(1, ATT_WIDTH)
        kn = jnp.tile(k_norm[l], N_HEADS).reshape(1, ATT_WIDTH)
        wa, wb, wo = (w[l].astype(BF16) for w in (w_branch_a, w_branch_b, w_out))

        oa_p, s_p = _hgrn_prompt(z, lb, g_out, tri, upper, nb, seq)
        k_p, v_p, qt, kh, vt, km = _qk_prompt(z, qn, kn, gmat, nb, nblk)
        kmh = km.reshape(nb, nblk, N_HEADS, HEAD_DIM).transpose(0, 2, 1, 3)
        kmh = jnp.pad(kmh, ((0, 0), (0, 0), (0, MAX_BLOCKS - nblk), (0, 0)))
        ob_p = _moba_prompt(slopes, qt, kh, vt, kmh).reshape(nb, ATT_WIDTH, seq)

        oa_s, s_s = _hgrn_sample(z, lb, g_out, state_hgrn, l, row_blk_s, db)
        q_s, k_s, v_s = _qk_sample(z, qn, kn, gmat, row_blk_s, db)
        km_s = _cache_mean(page_table, cache_k, l, db, n_blk_s)
        sel = _sample_select(q_s.reshape(db, N_HEADS, HEAD_DIM), km_s)[..., 0]
        sel_blk = sel.transpose(0, 2, 1)
        lpage = sel_blk[..., None] * PAGES_PER_BLOCK + jnp.arange(PAGES_PER_BLOCK)
        phys = jnp.take_along_axis(page_table, lpage.reshape(db, -1), axis=1)
        ob_s = _moba_sample(phys, sel_blk.reshape(db, -1), q_s, k_s, v_s, cache_k, cache_v, l, past_len)

        x = _merge(x, z, oa_p, ob_p, wa, wb, wo, tile=MOBA_BLOCK, row_blk0=0, n_tiles=nb * nblk,
                   ob_transposed=True, tiles_per_seq=nblk)
        x = _merge(x, z, oa_s, ob_s, wa, wb, wo, tile=db, row_blk0=row_blk_s, n_tiles=1,
                   ob_transposed=False)

        i = l // 2
        if l % 2 == 0:
            x = _ffn_dense(x, ffn_norm[l], w_dense_gate[i].astype(BF16), w_dense_up[i].astype(BF16),
                           w_dense_down[i].astype(BF16))
        else:
            x = _moe(x, ffn_norm[l], w_router[i], w_exp_gate[i].astype(BF16), w_exp_up[i].astype(BF16),
                     w_exp_down[i].astype(BF16))

        outs["kp"].append(k_p.reshape(nb, seq, N_HEADS, HEAD_DIM))
        outs["vp"].append(v_p.reshape(nb, seq, N_HEADS, HEAD_DIM))
        outs["sp"].append(s_p)
        outs["ks"].append(k_s.reshape(db, 1, N_HEADS, HEAD_DIM))
        outs["vs"].append(v_s.reshape(db, 1, N_HEADS, HEAD_DIM))
        outs["ss"].append(s_s)

    return (x[:n_rows_p].reshape(nb, seq, D_MODEL), x[n_rows_p:].reshape(db, 1, D_MODEL),
            jnp.stack(outs["kp"]), jnp.stack(outs["vp"]), jnp.stack(outs["sp"]),
            jnp.stack(outs["ks"]), jnp.stack(outs["vs"]), jnp.stack(outs["ss"]))
```

```python
import functools

import numpy as np
import jax
import jax.numpy as jnp
from jax import lax
from jax.experimental import pallas as pl
from jax.experimental.pallas import tpu as pltpu

F32 = jnp.float32
BF16 = jnp.bfloat16

D_MODEL = 1024
HG_WIDTH = 512
HG_HEAD_DIM = 128
HG_HEADS = HG_WIDTH // HG_HEAD_DIM
HG_CHUNK = 16
HEAD_DIM = 64
N_HEADS = 8
ATT_WIDTH = N_HEADS * HEAD_DIM
MOBA_BLOCK = 256
MOBA_TOPK = 3
PAGE_SIZE = 128
PAGES_PER_BLOCK = MOBA_BLOCK // PAGE_SIZE
D_FF = 2816
N_EXPERTS = 8
TOP_K = 2
RMS_EPS = 1e-6
IN_WIDTH = 4 * HG_WIDTH + 3 * ATT_WIDTH + 2 * D_MODEL
COL_ZQ, COL_ZF, COL_ZI, COL_ZG, COL_AQ, COL_AK, COL_AV = range(7)
COL_GA, COL_GB = 7, 9

KEY_WIDTH = 2 * HEAD_DIM
MAX_BLOCKS = 32
FEAT_OFF = MAX_BLOCKS + 3
FEAT_BLK = MAX_BLOCKS + 6
FEAT_END = MAX_BLOCKS + 12
LOG2E = 1.4426950408889634

MASKED = -1e30
VMEM_LIMIT = 56 * 1024 * 1024
FF_TILE = D_FF // 2
MOE_ROWS = 512
PAGE_GROUP = 16
LANES = 128


def _cparams(sem):
    return pltpu.CompilerParams(dimension_semantics=sem, vmem_limit_bytes=VMEM_LIMIT)


def _row_tile(n, hi):
    for t in range(hi, hi // 4, -16):
        if n % t == 0:
            return t
    return hi


def _split3(a):
    a1 = a.astype(BF16)
    r = a - a1.astype(F32)
    a2 = r.astype(BF16)
    a3 = (r - a2.astype(F32)).astype(BF16)
    return a1, a2, a3


def _dot(a, b):
    return jnp.dot(a, b, preferred_element_type=F32)


def _dot_nt(a, b):
    return lax.dot_general(a, b, (((1,), (1,)), ((), ())), preferred_element_type=F32)


def _dot_f32_lhs(a, b_exact):
    a1, a2, a3 = _split3(a)
    return _dot(a1, b_exact) + _dot(a2, b_exact) + _dot(a3, b_exact)


def _dot_f32_rhs(a_exact, b):
    b1, b2, b3 = _split3(b)
    return _dot(a_exact, b1) + _dot(a_exact, b2) + _dot(a_exact, b3)


def _rms(x, g):
    return x * lax.rsqrt(jnp.mean(x * x, axis=-1, keepdims=True) + RMS_EPS) * g


def _silu(x):
    return x / (1.0 + jnp.exp(-x))


def _sigmoid(x):
    return 1.0 / (1.0 + jnp.exp(-x))


def _top_mask_axis0(g, k):
    idx = lax.broadcasted_iota(jnp.int32, g.shape, 0)
    sel = jnp.zeros(g.shape, jnp.bool_)
    picks = []
    for _ in range(k):
        mx = jnp.max(g, axis=0, keepdims=True)
        first = jnp.min(jnp.where(g == mx, idx, g.shape[0]), axis=0, keepdims=True)
        hit = (idx == first) & (mx > -jnp.inf)
        sel = sel | hit
        g = jnp.where(idx == first, -jnp.inf, g)
        picks.append(first)
    return sel, picks


def _proj_kernel(x_ref, g_ref, w_ref, o_ref, h_scr):
    @pl.when(pl.program_id(1) == 0)
    def _():
        h_scr[...] = _rms(x_ref[...], g_ref[...]).astype(BF16)

    o_ref[...] = _dot(h_scr[...], w_ref[...])


def _proj(x, g, w_bf16):
    n = x.shape[0]
    tm = _row_tile(n, 1024)
    tn = IN_WIDTH // 4
    return pl.pallas_call(
        _proj_kernel,
        grid=(pl.cdiv(n, tm), IN_WIDTH // tn),
        in_specs=[pl.BlockSpec((tm, D_MODEL), lambda i, j: (i, 0)),
                  pl.BlockSpec((1, D_MODEL), lambda i, j: (0, 0)),
                  pl.BlockSpec((D_MODEL, tn), lambda i, j: (0, j))],
        out_specs=pl.BlockSpec((tm, tn), lambda i, j: (i, j)),
        out_shape=jax.ShapeDtypeStruct((n, IN_WIDTH), F32),
        scratch_shapes=[pltpu.VMEM((tm, D_MODEL), BF16)],
        compiler_params=_cparams(("parallel", "arbitrary")),
        name="in_proj",
    )(x, g.reshape(1, D_MODEL), w_bf16)


def _group_rms(x, gain, gmat):
    ms = _dot_f32_lhs(x * x, gmat) * (1.0 / HEAD_DIM)
    return x * lax.rsqrt(ms + RMS_EPS) * gain


def _qk_prompt_kernel(aq_ref, ak_ref, av_ref, qn_ref, kn_ref, gm_ref,
                      kt_out, vt_out, qt_out, kh_out, vtb_out, km_out):
    j = pl.program_id(1)
    gmat = gm_ref[...]
    q = _group_rms(aq_ref[...], qn_ref[...], gmat)
    k = _group_rms(ak_ref[...], kn_ref[...], gmat)
    v = av_ref[...]
    vt = v.T.reshape(N_HEADS, HEAD_DIM, MOBA_BLOCK)
    qt_out[0] = q.T.reshape(N_HEADS, HEAD_DIM, MOBA_BLOCK)
    kt_out[0] = k.T.reshape(N_HEADS, HEAD_DIM, MOBA_BLOCK)
    vt_out[0] = vt
    vtb_out[0, :, 0] = vt.astype(BF16)
    lane = lax.broadcasted_iota(jnp.int32, (MOBA_BLOCK, HEAD_DIM), 1)
    off = lax.broadcasted_iota(jnp.int32, (MOBA_BLOCK, HEAD_DIM), 0).astype(F32)
    feat = jnp.where(lane < FEAT_OFF, off,
                     jnp.where(lane < FEAT_BLK, j.astype(F32), jnp.where(lane < FEAT_END, 1.0, 0.0)))
    ext = jnp.where(lane < MAX_BLOCKS, (lane == j).astype(F32), feat).astype(BF16)
    kb = k.astype(BF16)
    for h in range(N_HEADS):
        kh_out[0, h, 0] = jnp.concatenate([kb[:, h * HEAD_DIM:(h + 1) * HEAD_DIM], ext], axis=1)
    km_out[0, 0] = jnp.sum(k, axis=0, keepdims=True) * (1.0 / MOBA_BLOCK)


def _qk_prompt(z, qn, kn, gmat, nb, nblk):
    seq = nblk * MOBA_BLOCK
    zspec = lambda c: pl.BlockSpec((MOBA_BLOCK, ATT_WIDTH), lambda b, i, c=c: (b * nblk + i, c))
    vec = pl.BlockSpec((1, ATT_WIDTH), lambda b, i: (0, 0))
    head5 = lambda r, c: pl.BlockSpec((1, N_HEADS, 1, r, c), lambda b, i: (b, 0, i, 0, 0))
    tspec = pl.BlockSpec((1, N_HEADS, HEAD_DIM, MOBA_BLOCK), lambda b, i: (b, 0, 0, i))
    tshape = jax.ShapeDtypeStruct((nb, N_HEADS, HEAD_DIM, seq), F32)
    return pl.pallas_call(
        _qk_prompt_kernel,
        grid=(nb, nblk),
        in_specs=[zspec(COL_AQ), zspec(COL_AK), zspec(COL_AV), vec, vec,
                  pl.BlockSpec((ATT_WIDTH, ATT_WIDTH), lambda b, i: (0, 0))],
        out_specs=[tspec, tspec, tspec,
                   head5(MOBA_BLOCK, KEY_WIDTH), head5(HEAD_DIM, MOBA_BLOCK),
                   pl.BlockSpec((1, 1, 1, ATT_WIDTH), lambda b, i: (b, i, 0, 0))],
        out_shape=[tshape, tshape, tshape,
                   jax.ShapeDtypeStruct((nb, N_HEADS, nblk, MOBA_BLOCK, KEY_WIDTH), BF16),
                   jax.ShapeDtypeStruct((nb, N_HEADS, nblk, HEAD_DIM, MOBA_BLOCK), BF16),
                   jax.ShapeDtypeStruct((nb, nblk, 1, ATT_WIDTH), F32)],
        compiler_params=_cparams(("parallel", "parallel")),
        name="qk_prompt",
    )(z, z, z, qn, kn, gmat)


def _qk_sample_kernel(aq_ref, ak_ref, av_ref, qn_ref, kn_ref, gm_ref, q_out, k_out, v_out):
    gmat = gm_ref[...]
    q_out[...] = _group_rms(aq_ref[...], qn_ref[...], gmat)
    k_out[...] = _group_rms(ak_ref[...], kn_ref[...], gmat)
    v_out[...] = av_ref[...]


def _qk_sample(z, qn, kn, gmat, row_blk, db):
    zspec = lambda c: pl.BlockSpec((db, ATT_WIDTH), lambda i, c=c: (row_blk, c))
    vec = pl.BlockSpec((1, ATT_WIDTH), lambda i: (0, 0))
    out = pl.BlockSpec((db, ATT_WIDTH), lambda i: (0, 0))
    shp = jax.ShapeDtypeStruct((db, ATT_WIDTH), F32)
    return pl.pallas_call(
        _qk_sample_kernel,
        grid=(1,),
        in_specs=[zspec(COL_AQ), zspec(COL_AK), zspec(COL_AV), vec, vec,
                  pl.BlockSpec((ATT_WIDTH, ATT_WIDTH), lambda i: (0, 0))],
        out_specs=[out, out, out],
        out_shape=[shp, shp, shp],
        compiler_params=_cparams(("arbitrary",)),
        name="qk_sample",
    )(z, z, z, qn, kn, gmat)


def _moba_prompt_kernel(slopes_ref, qt_ref, kh_ref, vt_ref, km_ref, o_ref,
                        qa_scr, s_scr, p_scr, m_scr, l_scr, acc_scr):
    i = pl.program_id(1)
    heads = range(N_HEADS)
    n_feat_rows = KEY_WIDTH - HEAD_DIM - MAX_BLOCKS
    qry_pos = lax.broadcasted_iota(jnp.int32, (1, MOBA_BLOCK), 1).astype(F32)
    feat_row = lax.broadcasted_iota(jnp.int32, (n_feat_rows, MOBA_BLOCK), 0)
    blk = lax.broadcasted_iota(jnp.int32, (MAX_BLOCKS, MOBA_BLOCK), 0)
    causal = (lax.broadcasted_iota(jnp.int32, (MOBA_BLOCK, MOBA_BLOCK), 0)
              <= lax.broadcasted_iota(jnp.int32, (MOBA_BLOCK, MOBA_BLOCK), 1))
    ones = jnp.ones((1, MOBA_BLOCK), F32)

    gates = [_dot(km_ref[0, h].astype(BF16), qt_ref[0, h].astype(BF16)) for h in heads]
    for h in heads:
        qt = qt_ref[0, h]
        sel, _ = _top_mask_axis0(jnp.where(blk < i, gates[h], -jnp.inf), MOBA_TOPK)
        sel_bias = jnp.where(sel | (blk >= i), 0.0, MASKED)
        c = slopes_ref[h] * LOG2E
        coefs = (c * ones, (c * MOBA_BLOCK) * ones, -c * qry_pos,
                 (-(c * MOBA_BLOCK) * i.astype(F32)) * ones)
        feat = jnp.zeros((n_feat_rows, MOBA_BLOCK), F32)
        for r, term in enumerate(t for v in coefs for t in _split3(v)):
            feat = jnp.where(feat_row == r, term.astype(F32), feat)
        q_aug = jnp.concatenate([(qt * (HEAD_DIM ** -0.5 * LOG2E)).astype(BF16), sel_bias.astype(BF16),
                                 feat.astype(BF16)], axis=0)
        qa_scr[h] = q_aug

    for h in heads:
        s_scr[h] = jnp.where(causal, _dot(kh_ref[0, h, i], qa_scr[h]), MASKED)
    for h in heads:
        s = s_scr[h]
        m = jnp.max(s, axis=0, keepdims=True)
        p = jnp.exp2(s - m)
        p_scr[h] = p.astype(BF16)
        m_scr[h] = m
        l_scr[h] = jnp.sum(p, axis=0, keepdims=True)
    for h in heads:
        acc_scr[h] = _dot(vt_ref[0, h, i], p_scr[h])

    def body(j, carry):
        for h in heads:
            s_scr[h] = _dot(kh_ref[0, h, j], qa_scr[h])
        m_old = [m_scr[h] for h in heads]
        l_old = [l_scr[h] for h in heads]
        m_new, l_new, alpha = [], [], []
        for h in heads:
            s = s_scr[h]
            m_new.append(jnp.maximum(m_old[h], jnp.max(s, axis=0, keepdims=True)))
            alpha.append(jnp.exp2(m_old[h] - m_new[h]))
            p = jnp.exp2(s - m_new[h])
            p_scr[h] = p.astype(BF16)
            l_new.append(alpha[h] * l_old[h] + jnp.sum(p, axis=0, keepdims=True))
        for h in heads:
            m_scr[h] = m_new[h]
            l_scr[h] = l_new[h]
        acc_old = [acc_scr[h] for h in heads]
        pv = [_dot(vt_ref[0, h, j], p_scr[h]) for h in heads]
        for h in heads:
            acc_scr[h] = alpha[h] * acc_old[h] + pv[h]
        return carry

    lax.fori_loop(0, i, body, 0)
    for h in heads:
        o_ref[0, h] = acc_scr[h] / l_scr[h]


def _moba_prompt(slopes, qt, kh, vt, kmh):
    nb, _, nblk = kh.shape[:3]
    resident = lambda shape: pl.BlockSpec(shape, lambda b, i, s: (b, 0, 0, 0, 0), pipeline_mode=pl.Buffered(1))
    return pl.pallas_call(
        _moba_prompt_kernel,
        grid_spec=pltpu.PrefetchScalarGridSpec(
            num_scalar_prefetch=1,
            grid=(nb, nblk),
            in_specs=[pl.BlockSpec((1, N_HEADS, HEAD_DIM, MOBA_BLOCK), lambda b, i, s: (b, 0, 0, i)),
                      resident((1, N_HEADS, nblk, MOBA_BLOCK, KEY_WIDTH)),
                      resident((1, N_HEADS, nblk, HEAD_DIM, MOBA_BLOCK)),
                      pl.BlockSpec((1, N_HEADS, MAX_BLOCKS, HEAD_DIM), lambda b, i, s: (b, 0, 0, 0))],
            out_specs=pl.BlockSpec((1, N_HEADS, HEAD_DIM, MOBA_BLOCK), lambda b, i, s: (b, 0, 0, i)),
            scratch_shapes=[pltpu.VMEM((N_HEADS, KEY_WIDTH, MOBA_BLOCK), BF16),
                            pltpu.VMEM((N_HEADS, MOBA_BLOCK, MOBA_BLOCK), F32),
                            pltpu.VMEM((N_HEADS, MOBA_BLOCK, MOBA_BLOCK), BF16),
                            pltpu.VMEM((N_HEADS, 1, MOBA_BLOCK), F32),
                            pltpu.VMEM((N_HEADS, 1, MOBA_BLOCK), F32),
                            pltpu.VMEM((N_HEADS, HEAD_DIM, MOBA_BLOCK), F32)]),
        out_shape=jax.ShapeDtypeStruct((nb, N_HEADS, HEAD_DIM, nblk * MOBA_BLOCK), F32),
        compiler_params=_cparams(("parallel", "arbitrary")),
        name="moba_prompt",
    )(slopes, qt, kh, vt, kmh)


def _hgrn_gates(zf, lb):
    log_sig = jnp.minimum(zf, 0.0) - jnp.log(1.0 + jnp.exp(-jnp.abs(zf)))
    a = jnp.log(lb)
    b = jnp.log(1.0 - lb) + log_sig
    logf = jnp.maximum(a, b) + jnp.log(1.0 + jnp.exp(-jnp.abs(a - b)))
    k = (1.0 - lb) / (1.0 + jnp.exp(zf))
    return logf, k


def _hgrn_prompt_kernel(zq_ref, zf_ref, zi_ref, zg_ref, lb_ref, go_ref, tri_ref, upper_ref,
                        o_ref, s_ref, st_scr, oi_scr):
    i = pl.program_id(1)
    tb = zq_ref.shape[0]

    @pl.when(i == 0)
    def _():
        st_scr[...] = jnp.zeros_like(st_scr)

    tri = tri_ref[...]
    upper = upper_ref[...]
    chunk = lax.broadcasted_iota(jnp.int32, (tb, HG_HEAD_DIM), 0) // HG_CHUNK
    head_cols = [slice(h * HG_HEAD_DIM, (h + 1) * HG_HEAD_DIM) for h in range(HG_HEADS)]

    heads = []
    for cols in head_cols:
        v = zi_ref[:, cols]
        logf, k = _hgrn_gates(zf_ref[:, cols], lb_ref[:, cols])
        b = _dot_f32_rhs(tri, logf)
        rest = _dot_f32_rhs(upper, logf)
        q_in = (zq_ref[:, cols] * jnp.exp(b)).astype(BF16)
        k_in = (k * jnp.exp(-b)).astype(BF16)
        k_out = k * jnp.exp(rest)
        dec = jnp.exp(b + rest)
        att = jnp.where(tri > 0, _dot_nt(q_in, k_in), 0.0).astype(BF16)
        o_intra = _dot(att, v.astype(BF16))
        heads.append((q_in, k_out, dec, v.T.astype(BF16), o_intra))

    sts = [st_scr[h] for h in range(HG_HEADS)]
    for n in range(tb // HG_CHUNK):
        rows = slice(n * HG_CHUNK, (n + 1) * HG_CHUNK)
        for h, cols in enumerate(head_cols):
            q_in, k_out, dec, vt, _ = heads[h]
            oi_scr[rows, cols] = _dot_nt(q_in[rows], sts[h].astype(BF16))
            k_n = jnp.where(chunk == n, k_out, 0.0).astype(BF16)
            sts[h] = sts[h] * dec[n * HG_CHUNK:n * HG_CHUNK + 1, :] + _dot(vt, k_n)

    go = go_ref[...]
    for h, cols in enumerate(head_cols):
        st_scr[h] = sts[h]
        o = heads[h][4] + oi_scr[:, cols]
        o_ref[:, cols] = _rms(o, go) * _silu(zg_ref[:, cols])

    @pl.when(i == pl.num_programs(1) - 1)
    def _():
        for h in range(HG_HEADS):
            s_ref[0, h] = sts[h].T


def _hgrn_prompt(z, lb, g_out, tri, upper, nb, seq):
    tb = MOBA_BLOCK
    nt = seq // tb
    zspec = lambda c: pl.BlockSpec((tb, HG_WIDTH), lambda b, i, c=c: (b * nt + i, c))
    sq = pl.BlockSpec((tb, tb), lambda b, i: (0, 0))
    return pl.pallas_call(
        _hgrn_prompt_kernel,
        grid=(nb, nt),
        in_specs=[zspec(COL_ZQ), zspec(COL_ZF), zspec(COL_ZI), zspec(COL_ZG),
                  pl.BlockSpec((1, HG_WIDTH), lambda b, i: (0, 0)),
                  pl.BlockSpec((1, HG_HEAD_DIM), lambda b, i: (0, 0)),
                  sq, sq],
        out_specs=[pl.BlockSpec((tb, HG_WIDTH), lambda b, i: (b * nt + i, 0)),
                   pl.BlockSpec((1, HG_HEADS, HG_HEAD_DIM, HG_HEAD_DIM), lambda b, i: (b, 0, 0, 0))],
        out_shape=[jax.ShapeDtypeStruct((nb * seq, HG_WIDTH), F32),
                   jax.ShapeDtypeStruct((nb, HG_HEADS, HG_HEAD_DIM, HG_HEAD_DIM), F32)],
        scratch_shapes=[pltpu.VMEM((HG_HEADS, HG_HEAD_DIM, HG_HEAD_DIM), F32),
                        pltpu.VMEM((tb, HG_WIDTH), F32)],
        compiler_params=_cparams(("parallel", "arbitrary")),
        name="hgrn_prompt",
    )(z, z, z, z, lb, g_out, tri, upper)


def _hgrn_sample_kernel(zq_ref, zf_ref, zi_ref, zg_ref, lb_ref, go_ref, s_ref, o_ref, sn_ref):
    db = zq_ref.shape[0]
    eye = (lax.broadcasted_iota(jnp.int32, (HG_HEAD_DIM, HG_HEAD_DIM), 0)
           == lax.broadcasted_iota(jnp.int32, (HG_HEAD_DIM, HG_HEAD_DIM), 1))
    col = lambda r: jnp.sum(jnp.where(eye, r, 0.0), axis=1, keepdims=True)
    lb = lb_ref[...]
    go = go_ref[...]

    def body(b, carry):
        row = pl.ds(b, 1)
        logf, k = _hgrn_gates(zf_ref[row, :], lb)
        s_new = s_ref[0, b, 0] * col(jnp.exp(logf)) + col(k) * zi_ref[row, :]
        sn_ref[b, 0] = s_new
        o = jnp.sum(s_new * col(zq_ref[row, :]), axis=0, keepdims=True)
        o_ref[row, :] = _rms(o, go) * _silu(zg_ref[row, :])
        return carry

    lax.fori_loop(0, db, body, 0)


def _hgrn_sample(z, lb, g_out, state, layer, row_blk, db):
    zspec = lambda c: pl.BlockSpec((db, HG_HEAD_DIM), lambda h, c=c: (row_blk, 4 * c + h))
    return pl.pallas_call(
        _hgrn_sample_kernel,
        grid=(HG_HEADS,),
        in_specs=[zspec(COL_ZQ), zspec(COL_ZF), zspec(COL_ZI), zspec(COL_ZG),
                  pl.BlockSpec((1, HG_HEAD_DIM), lambda h: (0, h)),
                  pl.BlockSpec((1, HG_HEAD_DIM), lambda h: (0, 0)),
                  pl.BlockSpec((1, db, 1, HG_HEAD_DIM, HG_HEAD_DIM), lambda h: (layer, 0, h, 0, 0))],
        out_specs=[pl.BlockSpec((db, HG_HEAD_DIM), lambda h: (0, h)),
                   pl.BlockSpec((db, 1, HG_HEAD_DIM, HG_HEAD_DIM), lambda h: (0, h, 0, 0))],
        out_shape=[jax.ShapeDtypeStruct((db, HG_WIDTH), F32),
                   jax.ShapeDtypeStruct((db, HG_HEADS, HG_HEAD_DIM, HG_HEAD_DIM), F32)],
        compiler_params=_cparams(("parallel",)),
        name="hgrn_sample",
    )(z, z, z, z, lb, g_out, state)


def _cache_mean_kernel(pt_ref, *refs):
    pages, o_ref = refs[:-1], refs[-1]
    g = pl.program_id(1)
    per_step = len(pages) // PAGES_PER_BLOCK
    ones = jnp.ones((8, PAGE_SIZE), BF16)
    for c in range(per_step):
        tot = pages[PAGES_PER_BLOCK * c][0, 0]
        for p in range(1, PAGES_PER_BLOCK):
            tot = tot + pages[PAGES_PER_BLOCK * c + p][0, 0]
        row = sum(_dot_nt(ones, t) for t in _split3(tot))[0:1]
        o_ref[0, pl.ds(g * per_step + c, 1), :] = row * (1.0 / MOBA_BLOCK)


def _cache_mean(page_table, cache_t, layer, db, n_blk):
    n_pages = n_blk * PAGES_PER_BLOCK
    pg = PAGE_GROUP if n_pages % PAGE_GROUP == 0 else PAGES_PER_BLOCK
    spec = lambda k: pl.BlockSpec((1, 1, ATT_WIDTH, PAGE_SIZE),
                                  lambda b, g, pt, k=k: (layer, pt[b, g * pg + k], 0, 0))
    return pl.pallas_call(
        _cache_mean_kernel,
        grid_spec=pltpu.PrefetchScalarGridSpec(
            num_scalar_prefetch=1,
            grid=(db, n_pages // pg),
            in_specs=[spec(k) for k in range(pg)],
            out_specs=pl.BlockSpec((1, n_blk, ATT_WIDTH), lambda b, g, pt: (b, 0, 0))),
        out_shape=jax.ShapeDtypeStruct((db, n_blk, ATT_WIDTH), F32),
        compiler_params=_cparams(("parallel", "arbitrary")),
        name="cache_block_mean",
    )(page_table, *([cache_t] * pg))


def _sample_select_kernel(q_ref, km_ref, hm_ref, o_ref):
    b = pl.program_id(0)
    km = km_ref[0].astype(BF16).astype(F32)
    q = q_ref[pl.ds(b, 1), :].astype(BF16).astype(F32)
    gate = _dot_f32_lhs(km * q, hm_ref[...])
    _, picks = _top_mask_axis0(gate, MOBA_TOPK)
    rows = picks + [jnp.zeros_like(picks[0])] * (o_ref.shape[1] - len(picks))
    o_ref[0] = jnp.concatenate(rows, axis=0)


def _sample_select(q_s, km_s, head_mat):
    db, n_blk, _ = km_s.shape
    return pl.pallas_call(
        _sample_select_kernel,
        grid=(db,),
        in_specs=[pl.BlockSpec((db, ATT_WIDTH), lambda b: (0, 0)),
                  pl.BlockSpec((1, n_blk, ATT_WIDTH), lambda b: (b, 0, 0)),
                  pl.BlockSpec((ATT_WIDTH, LANES), lambda b: (0, 0))],
        out_specs=pl.BlockSpec((1, 8, LANES), lambda b: (b, 0, 0)),
        out_shape=jax.ShapeDtypeStruct((db, 8, LANES), jnp.int32),
        compiler_params=_cparams(("parallel",)),
        name="sample_select",
    )(q_s, km_s, head_mat)


def _moba_sample_kernel(past_len, phys_ref, blk_ref, slopes_ref, q_ref, k_ref, v_ref, *refs):
    pages_per_head = MOBA_TOPK * PAGES_PER_BLOCK
    k_pages, v_pages, o_ref = refs[:pages_per_head], refs[pages_per_head:2 * pages_per_head], refs[-1]
    b = pl.program_id(0)
    h = pl.program_id(1)
    m_keys = MOBA_TOPK * MOBA_BLOCK
    head = pl.ds(h, 1)
    q = q_ref[0, head, :] * (HEAD_DIM ** -0.5)
    kt = jnp.concatenate([r[0, 0, 0] for r in k_pages], axis=1).astype(BF16)
    vt = jnp.concatenate([r[0, 0, 0] for r in v_pages], axis=1).astype(BF16)
    s = _dot(jnp.broadcast_to(q, (8, HEAD_DIM)).astype(BF16), kt)[0:1]
    lane = lax.broadcasted_iota(jnp.int32, (1, m_keys), 1)
    kblk = blk_ref[b, h * MOBA_TOPK + MOBA_TOPK - 1]
    for t in range(MOBA_TOPK - 2, -1, -1):
        kblk = jnp.where(lane < (t + 1) * MOBA_BLOCK, blk_ref[b, h * MOBA_TOPK + t], kblk)
    kpos = kblk * MOBA_BLOCK + lane % MOBA_BLOCK
    s = s - slopes_ref[h] * (past_len - kpos).astype(F32)
    s_self = jnp.sum(q * k_ref[0, head, :], axis=1, keepdims=True)
    m = jnp.maximum(jnp.max(s, axis=1, keepdims=True), s_self)
    p = jnp.exp(s - m)
    p_self = jnp.exp(s_self - m)
    l = jnp.sum(p, axis=1, keepdims=True) + p_self
    o = _dot_nt(jnp.broadcast_to(p, (8, m_keys)).astype(BF16), vt)[0:1] + p_self * v_ref[0, head, :]
    o_ref[0, 0] = o / l


def _moba_sample(phys, sel_blk, slopes, q_s, k_s, v_s, cache_kt, cache_vt, layer, past_len):
    db = q_s.shape[0]
    pages_per_head = MOBA_TOPK * PAGES_PER_BLOCK
    row = pl.BlockSpec((1, N_HEADS, HEAD_DIM), lambda b, h, ph, sb, sl: (b, 0, 0))
    page = lambda t: pl.BlockSpec((1, 1, 1, HEAD_DIM, PAGE_SIZE),
                                  lambda b, h, ph, sb, sl, t=t: (layer, ph[b, h * pages_per_head + t], h, 0, 0))
    pages = [page(t) for t in range(pages_per_head)]
    r3 = lambda a: a.reshape(db, N_HEADS, HEAD_DIM)
    out = pl.pallas_call(
        functools.partial(_moba_sample_kernel, past_len),
        grid_spec=pltpu.PrefetchScalarGridSpec(
            num_scalar_prefetch=3,
            grid=(db, N_HEADS),
            in_specs=[row, row, row] + pages + pages,
            out_specs=pl.BlockSpec((1, 1, 1, HEAD_DIM), lambda b, h, ph, sb, sl: (b, h, 0, 0))),
        out_shape=jax.ShapeDtypeStruct((db, N_HEADS, 1, HEAD_DIM), F32),
        compiler_params=_cparams(("parallel", "arbitrary")),
        name="moba_sample",
    )(phys, sel_blk, slopes, r3(q_s), r3(k_s), r3(v_s),
      *([cache_kt] * pages_per_head), *([cache_vt] * pages_per_head))
    return out.reshape(db, ATT_WIDTH)


def _merge_kernel(ob_transposed, x_ref, oa_ref, ob_ref, ga0, ga1, gb0, gb1, wa_ref, wb_ref, wo_ref, o_ref):
    oa = oa_ref[...].astype(BF16)
    ob = ob_ref[0].T if ob_transposed else ob_ref[...]
    a = _dot(oa, wa_ref[...])
    bm = _dot(ob.astype(BF16), wb_ref[...])
    half = D_MODEL // 2
    m0 = _sigmoid(ga0[...]) * a[:, :half] + _sigmoid(gb0[...]) * bm[:, :half]
    m1 = _sigmoid(ga1[...]) * a[:, half:] + _sigmoid(gb1[...]) * bm[:, half:]
    y = _dot(m0.astype(BF16), wo_ref[:half, :]) + _dot(m1.astype(BF16), wo_ref[half:, :])
    o_ref[...] = x_ref[...] + y


def _merge(x, z, o_a, o_b, wa, wb, wo, *, tile, row_blk0, n_tiles, ob_transposed, tiles_per_seq=None):
    rspec = lambda w, c: pl.BlockSpec((tile, w), lambda i, c=c: (row_blk0 + i, c))
    if ob_transposed:
        ob_spec = pl.BlockSpec((1, ATT_WIDTH, tile), lambda i: (i // tiles_per_seq, 0, i % tiles_per_seq))
    else:
        ob_spec = pl.BlockSpec((tile, ATT_WIDTH), lambda i: (i, 0))
    wspec = lambda r: pl.BlockSpec((r, D_MODEL), lambda i: (0, 0))
    return pl.pallas_call(
        functools.partial(_merge_kernel, ob_transposed),
        grid=(n_tiles,),
        in_specs=[rspec(D_MODEL, 0),
                  pl.BlockSpec((tile, HG_WIDTH), lambda i: (i, 0)), ob_spec,
                  rspec(512, COL_GA), rspec(512, COL_GA + 1), rspec(512, COL_GB), rspec(512, COL_GB + 1),
                  wspec(HG_WIDTH), wspec(ATT_WIDTH), wspec(D_MODEL)],
        out_specs=rspec(D_MODEL, 0),
        out_shape=jax.ShapeDtypeStruct(x.shape, F32),
        input_output_aliases={0: 0},
        compiler_params=_cparams(("parallel",)),
        name="merge_t" if ob_transposed else "merge",
    )(x, o_a, o_b, z, z, z, z, wa, wb, wo)


def _ffn_kernel(x_ref, g_ref, wg_ref, wu_ref, wd_ref, o_ref, h_scr, acc_scr):
    j = pl.program_id(1)

    @pl.when(j == 0)
    def _():
        h_scr[...] = _rms(x_ref[...], g_ref[...]).astype(BF16)
        acc_scr[...] = jnp.zeros_like(acc_scr)

    h = h_scr[...]
    a = _silu(_dot(h, wg_ref[...])) * _dot(h, wu_ref[...])
    acc_scr[...] += _dot(a.astype(BF16), wd_ref[...])

    @pl.when(j == pl.num_programs(1) - 1)
    def _():
        o_ref[...] = x_ref[...] + acc_scr[...]


def _ffn_dense(x, g, wg, wu, wd):
    n = x.shape[0]
    tm = _row_tile(n, 512)
    return pl.pallas_call(
        _ffn_kernel,
        grid=(pl.cdiv(n, tm), D_FF // FF_TILE),
        in_specs=[pl.BlockSpec((tm, D_MODEL), lambda i, j: (i, 0)),
                  pl.BlockSpec((1, D_MODEL), lambda i, j: (0, 0)),
                  pl.BlockSpec((D_MODEL, FF_TILE), lambda i, j: (0, j)),
                  pl.BlockSpec((D_MODEL, FF_TILE), lambda i, j: (0, j)),
                  pl.BlockSpec((FF_TILE, D_MODEL), lambda i, j: (j, 0))],
        out_specs=pl.BlockSpec((tm, D_MODEL), lambda i, j: (i, 0)),
        out_shape=jax.ShapeDtypeStruct((n, D_MODEL), F32),
        scratch_shapes=[pltpu.VMEM((tm, D_MODEL), BF16), pltpu.VMEM((tm, D_MODEL), F32)],
        compiler_params=_cparams(("parallel", "arbitrary")),
        name="ffn_dense",
    )(x, g.reshape(1, D_MODEL), wg, wu, wd)


def _router_kernel(x_ref, g_ref, wr_ref, h_ref, logit_ref):
    h = _rms(x_ref[...], g_ref[...]).astype(BF16)
    h_ref[...] = h
    logit_ref[...] = _dot(h, wr_ref[...].astype(BF16))


def _router(x, g, wr_pad):
    n = x.shape[0]
    tm = _row_tile(n, 1024)
    return pl.pallas_call(
        _router_kernel,
        grid=(pl.cdiv(n, tm),),
        in_specs=[pl.BlockSpec((tm, D_MODEL), lambda i: (i, 0)),
                  pl.BlockSpec((1, D_MODEL), lambda i: (0, 0)),
                  pl.BlockSpec((D_MODEL, LANES), lambda i: (0, 0))],
        out_specs=[pl.BlockSpec((tm, D_MODEL), lambda i: (i, 0)),
                   pl.BlockSpec((tm, LANES), lambda i: (i, 0))],
        out_shape=[jax.ShapeDtypeStruct((n, D_MODEL), BF16),
                   jax.ShapeDtypeStruct((n, LANES), F32)],
        compiler_params=_cparams(("parallel",)),
        name="moe_router",
    )(x, g.reshape(1, D_MODEL), wr_pad)


def _expert_kernel(be_ref, used_ref, x_ref, wg_ref, wu_ref, wd_ref, o_ref, acc_scr):
    i = pl.program_id(0)
    j = pl.program_id(1)

    @pl.when(j == 0)
    def _():
        acc_scr[...] = jnp.zeros_like(acc_scr)

    @pl.when(used_ref[i] > 0)
    def _():
        x = x_ref[...]
        a = _silu(_dot(x, wg_ref[0])) * _dot(x, wu_ref[0])
        acc_scr[...] += _dot(a.astype(BF16), wd_ref[0])

    @pl.when(j == pl.num_programs(1) - 1)
    def _():
        o_ref[...] = acc_scr[...]


def _experts(blk_e, blk_used, xb, weg, weu, wed):
    rows = xb.shape[0]
    return pl.pallas_call(
        _expert_kernel,
        grid_spec=pltpu.PrefetchScalarGridSpec(
            num_scalar_prefetch=2,
            grid=(rows // MOE_ROWS, D_FF // FF_TILE),
            in_specs=[pl.BlockSpec((MOE_ROWS, D_MODEL), lambda i, j, be, bu: (i, 0)),
                      pl.BlockSpec((1, D_MODEL, FF_TILE), lambda i, j, be, bu: (be[i], 0, j)),
                      pl.BlockSpec((1, D_MODEL, FF_TILE), lambda i, j, be, bu: (be[i], 0, j)),
                      pl.BlockSpec((1, FF_TILE, D_MODEL), lambda i, j, be, bu: (be[i], j, 0))],
            out_specs=pl.BlockSpec((MOE_ROWS, D_MODEL), lambda i, j, be, bu: (i, 0)),
            scratch_shapes=[pltpu.VMEM((MOE_ROWS, D_MODEL), F32)]),
        out_shape=jax.ShapeDtypeStruct((rows, D_MODEL), F32),
        compiler_params=_cparams(("parallel", "arbitrary")),
        name="moe_experts",
    )(blk_e, blk_used, xb, weg, weu, wed)


def _moe(x, g, wr, weg, weu, wed):
    n = x.shape[0]
    wr_pad = jnp.zeros((D_MODEL, LANES), F32).at[:, :N_EXPERTS].set(wr)
    h, logits = _router(x, g, wr_pad)
    top_l, top_e = lax.top_k(logits[:, :N_EXPERTS], TOP_K)
    gates = jax.nn.softmax(top_l, axis=-1)
    n_slots = n * TOP_K
    n_blocks = -(-(n_slots + N_EXPERTS * (MOE_ROWS - 1)) // MOE_ROWS)
    flat_e = top_e.reshape(-1)
    order = jnp.argsort(flat_e)
    sorted_e = flat_e[order]
    counts = jnp.bincount(flat_e, length=N_EXPERTS)
    padded = (counts + MOE_ROWS - 1) // MOE_ROWS * MOE_ROWS
    start_sorted = jnp.cumsum(counts) - counts
    end_pad = jnp.cumsum(padded)
    start_pad = end_pad - padded
    dest = (start_pad[sorted_e] + jnp.arange(n_slots) - start_sorted[sorted_e]).astype(jnp.int32)
    row_tok = jnp.full((n_blocks * MOE_ROWS,), n, jnp.int32).at[dest].set((order // TOP_K).astype(jnp.int32))
    blk_start = jnp.arange(n_blocks) * MOE_ROWS
    blk_e = jnp.minimum(jnp.searchsorted(end_pad, blk_start, side='right'), N_EXPERTS - 1).astype(jnp.int32)
    blk_used = (blk_start < end_pad[-1]).astype(jnp.int32)
    xb = jnp.concatenate([h, jnp.zeros((1, D_MODEL), BF16)], axis=0)[row_tok]
    yb = _experts(blk_e, blk_used, xb, weg, weu, wed)
    slot_row = jnp.zeros((n_slots,), jnp.int32).at[order].set(dest)
    y = jnp.sum(yb[slot_row].reshape(n, TOP_K, D_MODEL) * gates[..., None], axis=1)
    return x + y


def _chunk_masks(tb):
    t = np.arange(tb)
    same = (t[:, None] // HG_CHUNK) == (t[None, :] // HG_CHUNK)
    tri = same & (t[None, :] <= t[:, None])
    upper = same & (t[None, :] > t[:, None])
    return jnp.asarray(tri, BF16), jnp.asarray(upper, BF16)


def kernel(x_prompt, x_sample, cache_k, cache_v, state_hgrn, page_table, w_in, hg_lower, hg_norm, q_norm, k_norm, w_branch_a, w_branch_b, w_out, attn_norm, ffn_norm, w_dense_gate, w_dense_up, w_dense_down, w_router, w_exp_gate, w_exp_up, w_exp_down):
    nb, seq, _ = x_prompt.shape
    db, dec_seq, _ = x_sample.shape
    depth, n_pool = cache_k.shape[:2]
    n_pages = page_table.shape[1]
    past_len = n_pages * PAGE_SIZE
    assert dec_seq == 1 and seq % MOBA_BLOCK == 0 and past_len % MOBA_BLOCK == 0
    n_rows_p = nb * seq
    assert n_rows_p % db == 0 and db % 8 == 0
    nblk = seq // MOBA_BLOCK
    n_blk_s = past_len // MOBA_BLOCK
    assert MOBA_TOPK <= nblk <= MAX_BLOCKS and n_blk_s >= MOBA_TOPK
    row_blk_s = n_rows_p // db

    slopes = 2.0 ** (-8.0 * jnp.arange(1, N_HEADS + 1, dtype=F32) / N_HEADS)
    lb_cum = jnp.cumsum(jax.nn.softmax(hg_lower.astype(F32), axis=0), axis=0)
    lower = lb_cum - lb_cum[:1]

    lane = np.arange(ATT_WIDTH)
    gmat = jnp.asarray((lane[:, None] // HEAD_DIM) == (lane[None, :] // HEAD_DIM), BF16)
    head_mat = jnp.asarray((lane[:, None] // HEAD_DIM) == np.arange(LANES)[None, :], BF16)
    tri, upper = _chunk_masks(MOBA_BLOCK)
    cache_kt = cache_k.transpose(0, 1, 3, 4, 2)
    cache_vt = cache_v.transpose(0, 1, 3, 4, 2)
    cache_kt4 = cache_kt.reshape(depth, n_pool, ATT_WIDTH, PAGE_SIZE)

    x = jnp.concatenate([x_prompt.reshape(n_rows_p, D_MODEL), x_sample.reshape(db, D_MODEL)], axis=0)
    outs = {name: [] for name in ("kp", "vp", "sp", "ks", "vs", "ss")}
    for l in range(depth):
        z = _proj(x, attn_norm[l], w_in[l].astype(BF16))
        lb = lower[l].reshape(1, HG_WIDTH)
        g_out = hg_norm[l].reshape(1, HG_HEAD_DIM)
        qn = jnp.tile(q_norm[l], N_HEADS).reshape(1, ATT_WIDTH)
        kn = jnp.tile(k_norm[l], N_HEADS).reshape(1, ATT_WIDTH)
        wa, wb, wo = (w[l].astype(BF16) for w in (w_branch_a, w_branch_b, w_out))

        oa_p, s_p = _hgrn_prompt(z, lb, g_out, tri, upper, nb, seq)
        k_p, v_p, qt, kh, vt, km = _qk_prompt(z, qn, kn, gmat, nb, nblk)
        kmh = km.reshape(nb, nblk, N_HEADS, HEAD_DIM).transpose(0, 2, 1, 3)
        kmh = jnp.pad(kmh, ((0, 0), (0, 0), (0, MAX_BLOCKS - nblk), (0, 0)))
        ob_p = _moba_prompt(slopes, qt, kh, vt, kmh).reshape(nb, ATT_WIDTH, seq)

        oa_s, s_s = _hgrn_sample(z, lb, g_out, state_hgrn, l, row_blk_s, db)
        q_s, k_s, v_s = _qk_sample(z, qn, kn, gmat, row_blk_s, db)
        km_s = _cache_mean(page_table, cache_kt4, l, db, n_blk_s)
        sel = _sample_select(q_s, km_s, head_mat)[:, :MOBA_TOPK, :N_HEADS]
        sel_blk = sel.transpose(0, 2, 1)
        lpage = sel_blk[..., None] * PAGES_PER_BLOCK + jnp.arange(PAGES_PER_BLOCK)
        phys = jnp.take_along_axis(page_table, lpage.reshape(db, -1), axis=1)
        ob_s = _moba_sample(phys, sel_blk.reshape(db, -1), slopes, q_s, k_s, v_s, cache_kt, cache_vt,
                            l, past_len)

        x = _merge(x, z, oa_p, ob_p, wa, wb, wo, tile=MOBA_BLOCK, row_blk0=0, n_tiles=nb * nblk,
                   ob_transposed=True, tiles_per_seq=nblk)
        x = _merge(x, z, oa_s, ob_s, wa, wb, wo, tile=db, row_blk0=row_blk_s, n_tiles=1,
                   ob_transposed=False)

        i = l // 2
        if l % 2 == 0:
            x = _ffn_dense(x, ffn_norm[l], w_dense_gate[i].astype(BF16), w_dense_up[i].astype(BF16),
                           w_dense_down[i].astype(BF16))
        else:
            x = _moe(x, ffn_norm[l], w_router[i], w_exp_gate[i].astype(BF16), w_exp_up[i].astype(BF16),
                     w_exp_down[i].astype(BF16))

        outs["kp"].append(k_p)
        outs["vp"].append(v_p)
        outs["sp"].append(s_p)
        outs["ks"].append(k_s.reshape(db, 1, N_HEADS, HEAD_DIM))
        outs["vs"].append(v_s.reshape(db, 1, N_HEADS, HEAD_DIM))
        outs["ss"].append(s_s)

    seq_major = lambda rows: jnp.stack(rows).transpose(0, 1, 4, 2, 3)
    return (x[:n_rows_p].reshape(nb, seq, D_MODEL), x[n_rows_p:].reshape(db, 1, D_MODEL),
            seq_major(outs["kp"]), seq_major(outs["vp"]), jnp.stack(outs["sp"]),
            jnp.stack(outs["ks"]), jnp.stack(outs["vs"]), jnp.stack(outs["ss"]))
```

```python
import functools

import numpy as np
import jax
import jax.numpy as jnp
from jax import lax
from jax.experimental import pallas as pl
from jax.experimental.pallas import tpu as pltpu

F32 = jnp.float32
BF16 = jnp.bfloat16

D_MODEL = 1024
HG_WIDTH = 512
HG_HEAD_DIM = 128
HG_HEADS = HG_WIDTH // HG_HEAD_DIM
HG_CHUNK = 16
HEAD_DIM = 64
N_HEADS = 8
ATT_WIDTH = N_HEADS * HEAD_DIM
MOBA_BLOCK = 256
MOBA_TOPK = 3
PAGE_SIZE = 128
PAGES_PER_BLOCK = MOBA_BLOCK // PAGE_SIZE
D_FF = 2816
N_EXPERTS = 8
TOP_K = 2
RMS_EPS = 1e-6
IN_WIDTH = 4 * HG_WIDTH + 3 * ATT_WIDTH + 2 * D_MODEL
COL_ZQ, COL_ZF, COL_ZI, COL_ZG, COL_AQ, COL_AK, COL_AV = range(7)
COL_GA, COL_GB = 7, 9

KEY_WIDTH = 2 * HEAD_DIM
MAX_BLOCKS = 32
FEAT_OFF = MAX_BLOCKS + 3
FEAT_BLK = MAX_BLOCKS + 6
FEAT_END = MAX_BLOCKS + 12
LOG2E = 1.4426950408889634
VAL_ROWS = HEAD_DIM + 16

MASKED = -1e30
VMEM_LIMIT = 56 * 1024 * 1024
FF_TILE = D_FF // 2
MOE_ROWS = 512
PAGE_GROUP = 16
LANES = 128


def _cparams(sem):
    return pltpu.CompilerParams(dimension_semantics=sem, vmem_limit_bytes=VMEM_LIMIT)


def _row_tile(n, hi):
    for t in range(hi, hi // 4, -16):
        if n % t == 0:
            return t
    return hi


def _split3(a):
    a1 = a.astype(BF16)
    r = a - a1.astype(F32)
    a2 = r.astype(BF16)
    a3 = (r - a2.astype(F32)).astype(BF16)
    return a1, a2, a3


def _dot(a, b):
    return jnp.dot(a, b, preferred_element_type=F32)


def _dot_nt(a, b):
    return lax.dot_general(a, b, (((1,), (1,)), ((), ())), preferred_element_type=F32)


def _dot_f32_lhs(a, b_exact):
    a1, a2, a3 = _split3(a)
    return _dot(a1, b_exact) + _dot(a2, b_exact) + _dot(a3, b_exact)


def _dot_f32_rhs(a_exact, b):
    b1, b2, b3 = _split3(b)
    return _dot(a_exact, b1) + _dot(a_exact, b2) + _dot(a_exact, b3)


def _rms(x, g):
    return x * lax.rsqrt(jnp.mean(x * x, axis=-1, keepdims=True) + RMS_EPS) * g


def _silu(x):
    return x / (1.0 + jnp.exp(-x))


def _sigmoid(x):
    return 1.0 / (1.0 + jnp.exp(-x))


def _top_mask_axis0(g, k):
    idx = lax.broadcasted_iota(jnp.int32, g.shape, 0)
    sel = jnp.zeros(g.shape, jnp.bool_)
    picks = []
    for _ in range(k):
        mx = jnp.max(g, axis=0, keepdims=True)
        first = jnp.min(jnp.where(g == mx, idx, g.shape[0]), axis=0, keepdims=True)
        hit = (idx == first) & (mx > -jnp.inf)
        sel = sel | hit
        g = jnp.where(idx == first, -jnp.inf, g)
        picks.append(first)
    return sel, picks


def _proj_kernel(x_ref, g_ref, w_ref, o_ref, h_scr):
    @pl.when(pl.program_id(1) == 0)
    def _():
        h_scr[...] = _rms(x_ref[...], g_ref[...]).astype(BF16)

    o_ref[...] = _dot(h_scr[...], w_ref[...])


def _proj(x, g, w_bf16):
    n = x.shape[0]
    tm = _row_tile(n, 1024)
    tn = IN_WIDTH // 4
    return pl.pallas_call(
        _proj_kernel,
        grid=(pl.cdiv(n, tm), IN_WIDTH // tn),
        in_specs=[pl.BlockSpec((tm, D_MODEL), lambda i, j: (i, 0)),
                  pl.BlockSpec((1, D_MODEL), lambda i, j: (0, 0)),
                  pl.BlockSpec((D_MODEL, tn), lambda i, j: (0, j))],
        out_specs=pl.BlockSpec((tm, tn), lambda i, j: (i, j)),
        out_shape=jax.ShapeDtypeStruct((n, IN_WIDTH), F32),
        scratch_shapes=[pltpu.VMEM((tm, D_MODEL), BF16)],
        compiler_params=_cparams(("parallel", "arbitrary")),
        name="in_proj",
    )(x, g.reshape(1, D_MODEL), w_bf16)


def _group_rms(x, gain, gmat):
    ms = _dot_f32_lhs(x * x, gmat) * (1.0 / HEAD_DIM)
    return x * lax.rsqrt(ms + RMS_EPS) * gain


def _qk_prompt_kernel(aq_ref, ak_ref, av_ref, qn_ref, kn_ref, gm_ref,
                      kt_out, vt_out, qt_out, kh_out, vtb_out, km_out):
    j = pl.program_id(1)
    gmat = gm_ref[...]
    q = _group_rms(aq_ref[...], qn_ref[...], gmat)
    k = _group_rms(ak_ref[...], kn_ref[...], gmat)
    v = av_ref[...]
    vt = v.T.reshape(N_HEADS, HEAD_DIM, MOBA_BLOCK)
    qt_out[0] = q.T.reshape(N_HEADS, HEAD_DIM, MOBA_BLOCK)
    kt_out[0] = k.T.reshape(N_HEADS, HEAD_DIM, MOBA_BLOCK)
    vt_out[0] = vt
    ones = jnp.ones((N_HEADS, VAL_ROWS - HEAD_DIM, MOBA_BLOCK), BF16)
    vtb_out[0, :, 0] = jnp.concatenate([vt.astype(BF16), ones], axis=1)
    lane = lax.broadcasted_iota(jnp.int32, (MOBA_BLOCK, HEAD_DIM), 1)
    off = lax.broadcasted_iota(jnp.int32, (MOBA_BLOCK, HEAD_DIM), 0).astype(F32)
    feat = jnp.where(lane < FEAT_OFF, off,
                     jnp.where(lane < FEAT_BLK, j.astype(F32), jnp.where(lane < FEAT_END, 1.0, 0.0)))
    ext = jnp.where(lane < MAX_BLOCKS, (lane == j).astype(F32), feat).astype(BF16)
    kb = k.astype(BF16)
    for h in range(N_HEADS):
        kh_out[0, h, 0] = jnp.concatenate([kb[:, h * HEAD_DIM:(h + 1) * HEAD_DIM], ext], axis=1)
    km_out[0, 0] = jnp.sum(k, axis=0, keepdims=True) * (1.0 / MOBA_BLOCK)


def _qk_prompt(z, qn, kn, gmat, nb, nblk):
    seq = nblk * MOBA_BLOCK
    zspec = lambda c: pl.BlockSpec((MOBA_BLOCK, ATT_WIDTH), lambda b, i, c=c: (b * nblk + i, c))
    vec = pl.BlockSpec((1, ATT_WIDTH), lambda b, i: (0, 0))
    head5 = lambda r, c: pl.BlockSpec((1, N_HEADS, 1, r, c), lambda b, i: (b, 0, i, 0, 0))
    tspec = pl.BlockSpec((1, N_HEADS, HEAD_DIM, MOBA_BLOCK), lambda b, i: (b, 0, 0, i))
    tshape = jax.ShapeDtypeStruct((nb, N_HEADS, HEAD_DIM, seq), F32)
    return pl.pallas_call(
        _qk_prompt_kernel,
        grid=(nb, nblk),
        in_specs=[zspec(COL_AQ), zspec(COL_AK), zspec(COL_AV), vec, vec,
                  pl.BlockSpec((ATT_WIDTH, ATT_WIDTH), lambda b, i: (0, 0))],
        out_specs=[tspec, tspec, tspec,
                   head5(MOBA_BLOCK, KEY_WIDTH), head5(VAL_ROWS, MOBA_BLOCK),
                   pl.BlockSpec((1, 1, 1, ATT_WIDTH), lambda b, i: (b, i, 0, 0))],
        out_shape=[tshape, tshape, tshape,
                   jax.ShapeDtypeStruct((nb, N_HEADS, nblk, MOBA_BLOCK, KEY_WIDTH), BF16),
                   jax.ShapeDtypeStruct((nb, N_HEADS, nblk, VAL_ROWS, MOBA_BLOCK), BF16),
                   jax.ShapeDtypeStruct((nb, nblk, 1, ATT_WIDTH), F32)],
        compiler_params=_cparams(("parallel", "parallel")),
        name="qk_prompt",
    )(z, z, z, qn, kn, gmat)


def _qk_sample_kernel(aq_ref, ak_ref, av_ref, qn_ref, kn_ref, gm_ref, q_out, k_out, v_out):
    gmat = gm_ref[...]
    q_out[...] = _group_rms(aq_ref[...], qn_ref[...], gmat)
    k_out[...] = _group_rms(ak_ref[...], kn_ref[...], gmat)
    v_out[...] = av_ref[...]


def _qk_sample(z, qn, kn, gmat, row_blk, db):
    zspec = lambda c: pl.BlockSpec((db, ATT_WIDTH), lambda i, c=c: (row_blk, c))
    vec = pl.BlockSpec((1, ATT_WIDTH), lambda i: (0, 0))
    out = pl.BlockSpec((db, ATT_WIDTH), lambda i: (0, 0))
    shp = jax.ShapeDtypeStruct((db, ATT_WIDTH), F32)
    return pl.pallas_call(
        _qk_sample_kernel,
        grid=(1,),
        in_specs=[zspec(COL_AQ), zspec(COL_AK), zspec(COL_AV), vec, vec,
                  pl.BlockSpec((ATT_WIDTH, ATT_WIDTH), lambda i: (0, 0))],
        out_specs=[out, out, out],
        out_shape=[shp, shp, shp],
        compiler_params=_cparams(("arbitrary",)),
        name="qk_sample",
    )(z, z, z, qn, kn, gmat)


def _moba_prompt_kernel(slopes_ref, qt_ref, kh_ref, vt_ref, km_ref, o_ref,
                        qa_scr, s_scr, p_scr, m_scr, acc_scr):
    i = pl.program_id(1)
    heads = range(N_HEADS)
    n_feat_rows = KEY_WIDTH - HEAD_DIM - MAX_BLOCKS
    qry_pos = lax.broadcasted_iota(jnp.int32, (1, MOBA_BLOCK), 1).astype(F32)
    feat_row = lax.broadcasted_iota(jnp.int32, (n_feat_rows, MOBA_BLOCK), 0)
    blk = lax.broadcasted_iota(jnp.int32, (MAX_BLOCKS, MOBA_BLOCK), 0)
    causal = (lax.broadcasted_iota(jnp.int32, (MOBA_BLOCK, MOBA_BLOCK), 0)
              <= lax.broadcasted_iota(jnp.int32, (MOBA_BLOCK, MOBA_BLOCK), 1))
    ones = jnp.ones((1, MOBA_BLOCK), F32)

    gates = [_dot(km_ref[0, h].astype(BF16), qt_ref[0, h].astype(BF16)) for h in heads]
    for h in heads:
        qt = qt_ref[0, h]
        sel, _ = _top_mask_axis0(jnp.where(blk < i, gates[h], -jnp.inf), MOBA_TOPK)
        sel_bias = jnp.where(sel | (blk >= i), 0.0, MASKED)
        c = slopes_ref[h] * LOG2E
        coefs = (c * ones, (c * MOBA_BLOCK) * ones, -c * qry_pos,
                 (-(c * MOBA_BLOCK) * i.astype(F32)) * ones)
        feat = jnp.zeros((n_feat_rows, MOBA_BLOCK), F32)
        for r, term in enumerate(t for v in coefs for t in _split3(v)):
            feat = jnp.where(feat_row == r, term.astype(F32), feat)
        q_aug = jnp.concatenate([(qt * (HEAD_DIM ** -0.5 * LOG2E)).astype(BF16), sel_bias.astype(BF16),
                                 feat.astype(BF16)], axis=0)
        qa_scr[h] = q_aug

    for h in heads:
        s_scr[h] = jnp.where(causal, _dot(kh_ref[0, h, i], qa_scr[h]), MASKED)
    for h in heads:
        s = s_scr[h]
        m = jnp.max(s, axis=0, keepdims=True)
        p = jnp.exp2(s - m)
        p_scr[h] = p.astype(BF16)
        m_scr[h] = m
    for h in heads:
        acc_scr[h] = _dot(vt_ref[0, h, i], p_scr[h])

    def body(j, carry):
        for h in heads:
            s_scr[h] = _dot(kh_ref[0, h, j], qa_scr[h])
        m_old = [m_scr[h] for h in heads]
        m_new, alpha = [], []
        for h in heads:
            s = s_scr[h]
            m_new.append(jnp.maximum(m_old[h], jnp.max(s, axis=0, keepdims=True)))
            alpha.append(jnp.exp2(m_old[h] - m_new[h]))
            p_scr[h] = jnp.exp2(s - m_new[h]).astype(BF16)
        for h in heads:
            m_scr[h] = m_new[h]
        acc_old = [acc_scr[h] for h in heads]
        pv = [_dot(vt_ref[0, h, j], p_scr[h]) for h in heads]
        for h in heads:
            acc_scr[h] = alpha[h] * acc_old[h] + pv[h]
        return carry

    lax.fori_loop(0, i, body, 0)
    for h in heads:
        acc = acc_scr[h]
        o_ref[0, h] = acc[:HEAD_DIM] / acc[HEAD_DIM:HEAD_DIM + 1]


def _moba_prompt(slopes, qt, kh, vt, kmh):
    nb, _, nblk = kh.shape[:3]
    resident = lambda shape: pl.BlockSpec(shape, lambda b, i, s: (b, 0, 0, 0, 0), pipeline_mode=pl.Buffered(1))
    return pl.pallas_call(
        _moba_prompt_kernel,
        grid_spec=pltpu.PrefetchScalarGridSpec(
            num_scalar_prefetch=1,
            grid=(nb, nblk),
            in_specs=[pl.BlockSpec((1, N_HEADS, HEAD_DIM, MOBA_BLOCK), lambda b, i, s: (b, 0, 0, i)),
                      resident((1, N_HEADS, nblk, MOBA_BLOCK, KEY_WIDTH)),
                      resident((1, N_HEADS, nblk, VAL_ROWS, MOBA_BLOCK)),
                      pl.BlockSpec((1, N_HEADS, MAX_BLOCKS, HEAD_DIM), lambda b, i, s: (b, 0, 0, 0))],
            out_specs=pl.BlockSpec((1, N_HEADS, HEAD_DIM, MOBA_BLOCK), lambda b, i, s: (b, 0, 0, i)),
            scratch_shapes=[pltpu.VMEM((N_HEADS, KEY_WIDTH, MOBA_BLOCK), BF16),
                            pltpu.VMEM((N_HEADS, MOBA_BLOCK, MOBA_BLOCK), F32),
                            pltpu.VMEM((N_HEADS, MOBA_BLOCK, MOBA_BLOCK), BF16),
                            pltpu.VMEM((N_HEADS, 1, MOBA_BLOCK), F32),
                            pltpu.VMEM((N_HEADS, VAL_ROWS, MOBA_BLOCK), F32)]),
        out_shape=jax.ShapeDtypeStruct((nb, N_HEADS, HEAD_DIM, nblk * MOBA_BLOCK), F32),
        compiler_params=_cparams(("parallel", "arbitrary")),
        name="moba_prompt",
    )(slopes, qt, kh, vt, kmh)


def _hgrn_gates(zf, lb):
    log_sig = jnp.minimum(zf, 0.0) - jnp.log(1.0 + jnp.exp(-jnp.abs(zf)))
    a = jnp.log(lb)
    b = jnp.log(1.0 - lb) + log_sig
    logf = jnp.maximum(a, b) + jnp.log(1.0 + jnp.exp(-jnp.abs(a - b)))
    k = (1.0 - lb) / (1.0 + jnp.exp(zf))
    return logf, k


def _hgrn_prompt_kernel(zq_ref, zf_ref, zi_ref, zg_ref, lb_ref, go_ref, tri_ref, upper_ref,
                        o_ref, s_ref, st_scr, oi_scr):
    i = pl.program_id(1)
    tb = zq_ref.shape[0]

    @pl.when(i == 0)
    def _():
        st_scr[...] = jnp.zeros_like(st_scr)

    tri = tri_ref[...]
    upper = upper_ref[...]
    head_cols = [slice(h * HG_HEAD_DIM, (h + 1) * HG_HEAD_DIM) for h in range(HG_HEADS)]

    heads = []
    for cols in head_cols:
        v = zi_ref[:, cols]
        logf, k = _hgrn_gates(zf_ref[:, cols], lb_ref[:, cols])
        b = _dot_f32_rhs(tri, logf)
        rest = _dot_f32_rhs(upper, logf)
        q_in = (zq_ref[:, cols] * jnp.exp(b)).astype(BF16)
        k_in = (k * jnp.exp(-b)).astype(BF16)
        k_out = (k * jnp.exp(rest)).astype(BF16)
        dec = jnp.exp(b + rest)
        vb = v.astype(BF16)
        att = jnp.where(tri > 0, _dot_nt(q_in, k_in), 0.0).astype(BF16)
        o_intra = _dot(att, vb)
        heads.append((q_in, k_out, dec, vb, o_intra))

    sts = [st_scr[h] for h in range(HG_HEADS)]
    for n in range(tb // HG_CHUNK):
        rows = slice(n * HG_CHUNK, (n + 1) * HG_CHUNK)
        for h, cols in enumerate(head_cols):
            q_in, k_out, dec, vb, _ = heads[h]
            oi_scr[rows, cols] = _dot_nt(q_in[rows], sts[h].astype(BF16))
            update = lax.dot_general(vb[rows], k_out[rows], (((0,), (0,)), ((), ())),
                                     preferred_element_type=F32)
            sts[h] = sts[h] * dec[n * HG_CHUNK:n * HG_CHUNK + 1, :] + update

    go = go_ref[...]
    for h, cols in enumerate(head_cols):
        st_scr[h] = sts[h]
        o = heads[h][4] + oi_scr[:, cols]
        o_ref[:, cols] = _rms(o, go) * _silu(zg_ref[:, cols])

    @pl.when(i == pl.num_programs(1) - 1)
    def _():
        for h in range(HG_HEADS):
            s_ref[0, h] = sts[h].T


def _hgrn_prompt(z, lb, g_out, tri, upper, nb, seq):
    tb = MOBA_BLOCK
    nt = seq // tb
    zspec = lambda c: pl.BlockSpec((tb, HG_WIDTH), lambda b, i, c=c: (b * nt + i, c))
    sq = pl.BlockSpec((tb, tb), lambda b, i: (0, 0))
    return pl.pallas_call(
        _hgrn_prompt_kernel,
        grid=(nb, nt),
        in_specs=[zspec(COL_ZQ), zspec(COL_ZF), zspec(COL_ZI), zspec(COL_ZG),
                  pl.BlockSpec((1, HG_WIDTH), lambda b, i: (0, 0)),
                  pl.BlockSpec((1, HG_HEAD_DIM), lambda b, i: (0, 0)),
                  sq, sq],
        out_specs=[pl.BlockSpec((tb, HG_WIDTH), lambda b, i: (b * nt + i, 0)),
                   pl.BlockSpec((1, HG_HEADS, HG_HEAD_DIM, HG_HEAD_DIM), lambda b, i: (b, 0, 0, 0))],
        out_shape=[jax.ShapeDtypeStruct((nb * seq, HG_WIDTH), F32),
                   jax.ShapeDtypeStruct((nb, HG_HEADS, HG_HEAD_DIM, HG_HEAD_DIM), F32)],
        scratch_shapes=[pltpu.VMEM((HG_HEADS, HG_HEAD_DIM, HG_HEAD_DIM), F32),
                        pltpu.VMEM((tb, HG_WIDTH), F32)],
        compiler_params=_cparams(("parallel", "arbitrary")),
        name="hgrn_prompt",
    )(z, z, z, z, lb, g_out, tri, upper)


def _hgrn_sample_kernel(zq_ref, zf_ref, zi_ref, zg_ref, lb_ref, go_ref, s_ref, o_ref, sn_ref):
    db = zq_ref.shape[0]
    eye = (lax.broadcasted_iota(jnp.int32, (HG_HEAD_DIM, HG_HEAD_DIM), 0)
           == lax.broadcasted_iota(jnp.int32, (HG_HEAD_DIM, HG_HEAD_DIM), 1))
    col = lambda r: jnp.sum(jnp.where(eye, r, 0.0), axis=1, keepdims=True)
    lb = lb_ref[...]
    go = go_ref[...]

    def body(b, carry):
        row = pl.ds(b, 1)
        logf, k = _hgrn_gates(zf_ref[row, :], lb)
        s_new = s_ref[0, b, 0] * col(jnp.exp(logf)) + col(k) * zi_ref[row, :]
        sn_ref[b, 0] = s_new
        o = jnp.sum(s_new * col(zq_ref[row, :]), axis=0, keepdims=True)
        o_ref[row, :] = _rms(o, go) * _silu(zg_ref[row, :])
        return carry

    lax.fori_loop(0, db, body, 0)


def _hgrn_sample(z, lb, g_out, state, layer, row_blk, db):
    zspec = lambda c: pl.BlockSpec((db, HG_HEAD_DIM), lambda h, c=c: (row_blk, 4 * c + h))
    return pl.pallas_call(
        _hgrn_sample_kernel,
        grid=(HG_HEADS,),
        in_specs=[zspec(COL_ZQ), zspec(COL_ZF), zspec(COL_ZI), zspec(COL_ZG),
                  pl.BlockSpec((1, HG_HEAD_DIM), lambda h: (0, h)),
                  pl.BlockSpec((1, HG_HEAD_DIM), lambda h: (0, 0)),
                  pl.BlockSpec((1, db, 1, HG_HEAD_DIM, HG_HEAD_DIM), lambda h: (layer, 0, h, 0, 0))],
        out_specs=[pl.BlockSpec((db, HG_HEAD_DIM), lambda h: (0, h)),
                   pl.BlockSpec((db, 1, HG_HEAD_DIM, HG_HEAD_DIM), lambda h: (0, h, 0, 0))],
        out_shape=[jax.ShapeDtypeStruct((db, HG_WIDTH), F32),
                   jax.ShapeDtypeStruct((db, HG_HEADS, HG_HEAD_DIM, HG_HEAD_DIM), F32)],
        compiler_params=_cparams(("parallel",)),
        name="hgrn_sample",
    )(z, z, z, z, lb, g_out, state)


def _cache_mean_kernel(pt_ref, *refs):
    pages, o_ref = refs[:-1], refs[-1]
    g = pl.program_id(1)
    per_step = len(pages) // PAGES_PER_BLOCK
    ones = jnp.ones((8, PAGE_SIZE), BF16)
    for c in range(per_step):
        tot = pages[PAGES_PER_BLOCK * c][0, 0]
        for p in range(1, PAGES_PER_BLOCK):
            tot = tot + pages[PAGES_PER_BLOCK * c + p][0, 0]
        row = sum(_dot_nt(ones, t) for t in _split3(tot))[0:1]
        o_ref[0, pl.ds(g * per_step + c, 1), :] = row * (1.0 / MOBA_BLOCK)


def _cache_mean(page_table, cache_t, layer, db, n_blk):
    n_pages = n_blk * PAGES_PER_BLOCK
    pg = PAGE_GROUP if n_pages % PAGE_GROUP == 0 else PAGES_PER_BLOCK
    spec = lambda k: pl.BlockSpec((1, 1, ATT_WIDTH, PAGE_SIZE),
                                  lambda b, g, pt, k=k: (layer, pt[b, g * pg + k], 0, 0))
    return pl.pallas_call(
        _cache_mean_kernel,
        grid_spec=pltpu.PrefetchScalarGridSpec(
            num_scalar_prefetch=1,
            grid=(db, n_pages // pg),
            in_specs=[spec(k) for k in range(pg)],
            out_specs=pl.BlockSpec((1, n_blk, ATT_WIDTH), lambda b, g, pt: (b, 0, 0))),
        out_shape=jax.ShapeDtypeStruct((db, n_blk, ATT_WIDTH), F32),
        compiler_params=_cparams(("parallel", "arbitrary")),
        name="cache_block_mean",
    )(page_table, *([cache_t] * pg))


def _sample_select_kernel(q_ref, km_ref, hm_ref, o_ref):
    b = pl.program_id(0)
    km = km_ref[0].astype(BF16).astype(F32)
    q = q_ref[pl.ds(b, 1), :].astype(BF16).astype(F32)
    gate = _dot_f32_lhs(km * q, hm_ref[...])
    _, picks = _top_mask_axis0(gate, MOBA_TOPK)
    rows = picks + [jnp.zeros_like(picks[0])] * (o_ref.shape[1] - len(picks))
    o_ref[0] = jnp.concatenate(rows, axis=0)


def _sample_select(q_s, km_s, head_mat):
    db, n_blk, _ = km_s.shape
    return pl.pallas_call(
        _sample_select_kernel,
        grid=(db,),
        in_specs=[pl.BlockSpec((db, ATT_WIDTH), lambda b: (0, 0)),
                  pl.BlockSpec((1, n_blk, ATT_WIDTH), lambda b: (b, 0, 0)),
                  pl.BlockSpec((ATT_WIDTH, LANES), lambda b: (0, 0))],
        out_specs=pl.BlockSpec((1, 8, LANES), lambda b: (b, 0, 0)),
        out_shape=jax.ShapeDtypeStruct((db, 8, LANES), jnp.int32),
        compiler_params=_cparams(("parallel",)),
        name="sample_select",
    )(q_s, km_s, head_mat)


def _moba_sample_kernel(past_len, phys_ref, blk_ref, slopes_ref, q_ref, k_ref, v_ref, *refs):
    pages_per_head = MOBA_TOPK * PAGES_PER_BLOCK
    k_pages, v_pages, o_ref = refs[:pages_per_head], refs[pages_per_head:2 * pages_per_head], refs[-1]
    b = pl.program_id(0)
    h = pl.program_id(1)
    m_keys = MOBA_TOPK * MOBA_BLOCK
    head = pl.ds(h, 1)
    q = q_ref[0, head, :] * (HEAD_DIM ** -0.5)
    kt = jnp.concatenate([r[0, 0, 0] for r in k_pages], axis=1).astype(BF16)
    vt = jnp.concatenate([r[0, 0, 0] for r in v_pages], axis=1).astype(BF16)
    s = _dot(jnp.broadcast_to(q, (8, HEAD_DIM)).astype(BF16), kt)[0:1]
    lane = lax.broadcasted_iota(jnp.int32, (1, m_keys), 1)
    kblk = blk_ref[b, h * MOBA_TOPK + MOBA_TOPK - 1]
    for t in range(MOBA_TOPK - 2, -1, -1):
        kblk = jnp.where(lane < (t + 1) * MOBA_BLOCK, blk_ref[b, h * MOBA_TOPK + t], kblk)
    kpos = kblk * MOBA_BLOCK + lane % MOBA_BLOCK
    s = s - slopes_ref[h] * (past_len - kpos).astype(F32)
    s_self = jnp.sum(q * k_ref[0, head, :], axis=1, keepdims=True)
    m = jnp.maximum(jnp.max(s, axis=1, keepdims=True), s_self)
    p = jnp.exp(s - m)
    p_self = jnp.exp(s_self - m)
    l = jnp.sum(p, axis=1, keepdims=True) + p_self
    o = _dot_nt(jnp.broadcast_to(p, (8, m_keys)).astype(BF16), vt)[0:1] + p_self * v_ref[0, head, :]
    o_ref[0, 0] = o / l


def _moba_sample(phys, sel_blk, slopes, q_s, k_s, v_s, cache_kt, cache_vt, layer, past_len):
    db = q_s.shape[0]
    pages_per_head = MOBA_TOPK * PAGES_PER_BLOCK
    row = pl.BlockSpec((1, N_HEADS, HEAD_DIM), lambda b, h, ph, sb, sl: (b, 0, 0))
    page = lambda t: pl.BlockSpec((1, 1, 1, HEAD_DIM, PAGE_SIZE),
                                  lambda b, h, ph, sb, sl, t=t: (layer, ph[b, h * pages_per_head + t], h, 0, 0))
    pages = [page(t) for t in range(pages_per_head)]
    r3 = lambda a: a.reshape(db, N_HEADS, HEAD_DIM)
    out = pl.pallas_call(
        functools.partial(_moba_sample_kernel, past_len),
        grid_spec=pltpu.PrefetchScalarGridSpec(
            num_scalar_prefetch=3,
            grid=(db, N_HEADS),
            in_specs=[row, row, row] + pages + pages,
            out_specs=pl.BlockSpec((1, 1, 1, HEAD_DIM), lambda b, h, ph, sb, sl: (b, h, 0, 0))),
        out_shape=jax.ShapeDtypeStruct((db, N_HEADS, 1, HEAD_DIM), F32),
        compiler_params=_cparams(("parallel", "arbitrary")),
        name="moba_sample",
    )(phys, sel_blk, slopes, r3(q_s), r3(k_s), r3(v_s),
      *([cache_kt] * pages_per_head), *([cache_vt] * pages_per_head))
    return out.reshape(db, ATT_WIDTH)


def _merge_kernel(ob_transposed, x_ref, oa_ref, ob_ref, ga0, ga1, gb0, gb1, wa_ref, wb_ref, wo_ref, o_ref):
    oa = oa_ref[...].astype(BF16)
    ob = ob_ref[0].T if ob_transposed else ob_ref[...]
    a = _dot(oa, wa_ref[...])
    bm = _dot(ob.astype(BF16), wb_ref[...])
    half = D_MODEL // 2
    m0 = _sigmoid(ga0[...]) * a[:, :half] + _sigmoid(gb0[...]) * bm[:, :half]
    m1 = _sigmoid(ga1[...]) * a[:, half:] + _sigmoid(gb1[...]) * bm[:, half:]
    y = _dot(m0.astype(BF16), wo_ref[:half, :]) + _dot(m1.astype(BF16), wo_ref[half:, :])
    o_ref[...] = x_ref[...] + y


def _merge(x, z, o_a, o_b, wa, wb, wo, *, tile, row_blk0, n_tiles, ob_transposed, tiles_per_seq=None):
    rspec = lambda w, c: pl.BlockSpec((tile, w), lambda i, c=c: (row_blk0 + i, c))
    if ob_transposed:
        ob_spec = pl.BlockSpec((1, ATT_WIDTH, tile), lambda i: (i // tiles_per_seq, 0, i % tiles_per_seq))
    else:
        ob_spec = pl.BlockSpec((tile, ATT_WIDTH), lambda i: (i, 0))
    wspec = lambda r: pl.BlockSpec((r, D_MODEL), lambda i: (0, 0))
    return pl.pallas_call(
        functools.partial(_merge_kernel, ob_transposed),
        grid=(n_tiles,),
        in_specs=[rspec(D_MODEL, 0),
                  pl.BlockSpec((tile, HG_WIDTH), lambda i: (i, 0)), ob_spec,
                  rspec(512, COL_GA), rspec(512, COL_GA + 1), rspec(512, COL_GB), rspec(512, COL_GB + 1),
                  wspec(HG_WIDTH), wspec(ATT_WIDTH), wspec(D_MODEL)],
        out_specs=rspec(D_MODEL, 0),
        out_shape=jax.ShapeDtypeStruct(x.shape, F32),
        input_output_aliases={0: 0},
        compiler_params=_cparams(("parallel",)),
        name="merge_t" if ob_transposed else "merge",
    )(x, o_a, o_b, z, z, z, z, wa, wb, wo)


def _ffn_kernel(x_ref, g_ref, wg_ref, wu_ref, wd_ref, o_ref, h_scr, acc_scr):
    j = pl.program_id(1)

    @pl.when(j == 0)
    def _():
        h_scr[...] = _rms(x_ref[...], g_ref[...]).astype(BF16)
        acc_scr[...] = jnp.zeros_like(acc_scr)

    h = h_scr[...]
    a = _silu(_dot(h, wg_ref[...])) * _dot(h, wu_ref[...])
    acc_scr[...] += _dot(a.astype(BF16), wd_ref[...])

    @pl.when(j == pl.num_programs(1) - 1)
    def _():
        o_ref[...] = x_ref[...] + acc_scr[...]


def _ffn_dense(x, g, wg, wu, wd):
    n = x.shape[0]
    tm = _row_tile(n, 512)
    return pl.pallas_call(
        _ffn_kernel,
        grid=(pl.cdiv(n, tm), D_FF // FF_TILE),
        in_specs=[pl.BlockSpec((tm, D_MODEL), lambda i, j: (i, 0)),
                  pl.BlockSpec((1, D_MODEL), lambda i, j: (0, 0)),
                  pl.BlockSpec((D_MODEL, FF_TILE), lambda i, j: (0, j)),
                  pl.BlockSpec((D_MODEL, FF_TILE), lambda i, j: (0, j)),
                  pl.BlockSpec((FF_TILE, D_MODEL), lambda i, j: (j, 0))],
        out_specs=pl.BlockSpec((tm, D_MODEL), lambda i, j: (i, 0)),
        out_shape=jax.ShapeDtypeStruct((n, D_MODEL), F32),
        scratch_shapes=[pltpu.VMEM((tm, D_MODEL), BF16), pltpu.VMEM((tm, D_MODEL), F32)],
        compiler_params=_cparams(("parallel", "arbitrary")),
        name="ffn_dense",
    )(x, g.reshape(1, D_MODEL), wg, wu, wd)


def _router_kernel(x_ref, g_ref, wr_ref, h_ref, logit_ref):
    h = _rms(x_ref[...], g_ref[...])
    h_ref[...] = h
    logit_ref[...] = _dot(h.astype(BF16), wr_ref[...].astype(BF16))


def _router(x, g, wr_pad):
    n = x.shape[0]
    tm = _row_tile(n, 1024)
    return pl.pallas_call(
        _router_kernel,
        grid=(pl.cdiv(n, tm),),
        in_specs=[pl.BlockSpec((tm, D_MODEL), lambda i: (i, 0)),
                  pl.BlockSpec((1, D_MODEL), lambda i: (0, 0)),
                  pl.BlockSpec((D_MODEL, LANES), lambda i: (0, 0))],
        out_specs=[pl.BlockSpec((tm, D_MODEL), lambda i: (i, 0)),
                   pl.BlockSpec((tm, LANES), lambda i: (i, 0))],
        out_shape=[jax.ShapeDtypeStruct((n, D_MODEL), F32),
                   jax.ShapeDtypeStruct((n, LANES), F32)],
        compiler_params=_cparams(("parallel",)),
        name="moe_router",
    )(x, g.reshape(1, D_MODEL), wr_pad)


def _expert_kernel(be_ref, used_ref, x_ref, wg_ref, wu_ref, wd_ref, o_ref, acc_scr):
    i = pl.program_id(0)
    j = pl.program_id(1)

    @pl.when(j == 0)
    def _():
        acc_scr[...] = jnp.zeros_like(acc_scr)

    @pl.when(used_ref[i] > 0)
    def _():
        x = x_ref[...].astype(BF16)
        a = _silu(_dot(x, wg_ref[0])) * _dot(x, wu_ref[0])
        acc_scr[...] += _dot(a.astype(BF16), wd_ref[0])

    @pl.when(j == pl.num_programs(1) - 1)
    def _():
        o_ref[...] = acc_scr[...]


def _experts(blk_e, blk_used, xb, weg, weu, wed):
    rows = xb.shape[0]
    return pl.pallas_call(
        _expert_kernel,
        grid_spec=pltpu.PrefetchScalarGridSpec(
            num_scalar_prefetch=2,
            grid=(rows // MOE_ROWS, D_FF // FF_TILE),
            in_specs=[pl.BlockSpec((MOE_ROWS, D_MODEL), lambda i, j, be, bu: (i, 0)),
                      pl.BlockSpec((1, D_MODEL, FF_TILE), lambda i, j, be, bu: (be[i], 0, j)),
                      pl.BlockSpec((1, D_MODEL, FF_TILE), lambda i, j, be, bu: (be[i], 0, j)),
                      pl.BlockSpec((1, FF_TILE, D_MODEL), lambda i, j, be, bu: (be[i], j, 0))],
            out_specs=pl.BlockSpec((MOE_ROWS, D_MODEL), lambda i, j, be, bu: (i, 0)),
            scratch_shapes=[pltpu.VMEM((MOE_ROWS, D_MODEL), F32)]),
        out_shape=jax.ShapeDtypeStruct((rows, D_MODEL), F32),
        compiler_params=_cparams(("parallel", "arbitrary")),
        name="moe_experts",
    )(blk_e, blk_used, xb, weg, weu, wed)


def _row_copy(src, src_row, dst, dst_row, sem):
    return pltpu.make_async_copy(src.at[pl.ds(src_row, 1)], dst.at[pl.ds(dst_row, 1)], sem)


def _dispatch_kernel(n_tokens, dest_ref, h_ref, xb_in_ref, xb_ref, sem):
    del xb_in_ref
    tile = dest_ref.shape[2] // TOP_K
    tok0 = pl.program_id(0) * tile
    n_here = jnp.minimum(tile, n_tokens - tok0)

    def start(r, carry):
        for k in range(TOP_K):
            _row_copy(h_ref, tok0 + r, xb_ref, dest_ref[0, 0, r * TOP_K + k], sem).start()
        return carry

    def wait(r, carry):
        for k in range(TOP_K):
            _row_copy(h_ref, tok0 + r, xb_ref, dest_ref[0, 0, r * TOP_K + k], sem).wait()
        return carry

    lax.fori_loop(0, n_here, start, 0)
    lax.fori_loop(0, n_here, wait, 0)


def _dispatch(dest, h, n_rows):
    n = h.shape[0]
    tile = MOE_ROWS
    n_tiles = pl.cdiv(n, tile)
    dest_p = jnp.pad(dest, ((0, n_tiles * tile - n), (0, 0))).reshape(n_tiles, 1, tile * TOP_K)
    hbm = pl.BlockSpec(memory_space=pl.ANY)
    return pl.pallas_call(
        functools.partial(_dispatch_kernel, n),
        grid=(n_tiles,),
        in_specs=[pl.BlockSpec((1, 1, tile * TOP_K), lambda i: (i, 0, 0), memory_space=pltpu.SMEM), hbm, hbm],
        out_specs=hbm,
        out_shape=jax.ShapeDtypeStruct((n_rows, D_MODEL), F32),
        scratch_shapes=[pltpu.SemaphoreType.DMA(())],
        input_output_aliases={2: 0},
        compiler_params=_cparams(("arbitrary",)),
        name="moe_dispatch",
    )(dest_p, h, jnp.zeros((n_rows, D_MODEL), F32))


def _combine_kernel(slot_ref, x_ref, gate_ref, yb_ref, o_ref, ybuf, sem):
    tile = x_ref.shape[0]

    def start(r, carry):
        for k in range(TOP_K):
            _row_copy(yb_ref, slot_ref[0, 0, r * TOP_K + k], ybuf.at[k], r, sem).start()
        return carry

    def wait(r, carry):
        for k in range(TOP_K):
            _row_copy(yb_ref, slot_ref[0, 0, r * TOP_K + k], ybuf.at[k], r, sem).wait()
        return carry

    lax.fori_loop(0, tile, start, 0)
    lax.fori_loop(0, tile, wait, 0)
    y = x_ref[...]
    for k in range(TOP_K):
        y = y + gate_ref[:, k:k + 1] * ybuf[k]
    o_ref[...] = y


def _combine(x, gates, dest, yb):
    n = x.shape[0]
    tile = _row_tile(n, 512)
    n_tiles = pl.cdiv(n, tile)
    slot_p = jnp.pad(dest, ((0, n_tiles * tile - n), (0, 0))).reshape(n_tiles, 1, tile * TOP_K)
    return pl.pallas_call(
        _combine_kernel,
        grid=(n_tiles,),
        in_specs=[pl.BlockSpec((1, 1, tile * TOP_K), lambda i: (i, 0, 0), memory_space=pltpu.SMEM),
                  pl.BlockSpec((tile, D_MODEL), lambda i: (i, 0)),
                  pl.BlockSpec((tile, TOP_K), lambda i: (i, 0)),
                  pl.BlockSpec(memory_space=pl.ANY)],
        out_specs=pl.BlockSpec((tile, D_MODEL), lambda i: (i, 0)),
        out_shape=jax.ShapeDtypeStruct((n, D_MODEL), F32),
        scratch_shapes=[pltpu.VMEM((TOP_K, tile, D_MODEL), F32), pltpu.SemaphoreType.DMA(())],
        compiler_params=_cparams(("arbitrary",)),
        name="moe_combine",
    )(slot_p, x, gates, yb)


def _moe(x, g, wr, weg, weu, wed):
    n = x.shape[0]
    wr_pad = jnp.zeros((D_MODEL, LANES), F32).at[:, :N_EXPERTS].set(wr)
    h, logits = _router(x, g, wr_pad)
    top_l, top_e = lax.top_k(logits[:, :N_EXPERTS], TOP_K)
    gates = jax.nn.softmax(top_l, axis=-1)
    n_slots = n * TOP_K
    n_blocks = -(-(n_slots + N_EXPERTS * (MOE_ROWS - 1)) // MOE_ROWS)
    onehot = (top_e.reshape(n_slots, 1) == jnp.arange(N_EXPERTS)).astype(jnp.int32)
    upto = jnp.cumsum(onehot, axis=0)
    counts = upto[-1]
    padded = (counts + MOE_ROWS - 1) // MOE_ROWS * MOE_ROWS
    end_pad = jnp.cumsum(padded)
    start_pad = end_pad - padded
    dest = jnp.sum(onehot * (start_pad + upto - 1), axis=1).astype(jnp.int32).reshape(n, TOP_K)
    blk_start = jnp.arange(n_blocks) * MOE_ROWS
    blk_e = jnp.minimum(jnp.searchsorted(end_pad, blk_start, side='right'), N_EXPERTS - 1).astype(jnp.int32)
    blk_used = (blk_start < end_pad[-1]).astype(jnp.int32)
    xb = _dispatch(dest, h, n_blocks * MOE_ROWS)
    yb = _experts(blk_e, blk_used, xb, weg, weu, wed)
    return _combine(x, gates, dest, yb)


def _chunk_masks(tb):
    t = np.arange(tb)
    same = (t[:, None] // HG_CHUNK) == (t[None, :] // HG_CHUNK)
    tri = same & (t[None, :] <= t[:, None])
    upper = same & (t[None, :] > t[:, None])
    return jnp.asarray(tri, BF16), jnp.asarray(upper, BF16)


def kernel(x_prompt, x_sample, cache_k, cache_v, state_hgrn, page_table, w_in, hg_lower, hg_norm, q_norm, k_norm, w_branch_a, w_branch_b, w_out, attn_norm, ffn_norm, w_dense_gate, w_dense_up, w_dense_down, w_router, w_exp_gate, w_exp_up, w_exp_down):
    nb, seq, _ = x_prompt.shape
    db, dec_seq, _ = x_sample.shape
    depth, n_pool = cache_k.shape[:2]
    n_pages = page_table.shape[1]
    past_len = n_pages * PAGE_SIZE
    assert dec_seq == 1 and seq % MOBA_BLOCK == 0 and past_len % MOBA_BLOCK == 0
    n_rows_p = nb * seq
    assert n_rows_p % db == 0 and db % 8 == 0
    nblk = seq // MOBA_BLOCK
    n_blk_s = past_len // MOBA_BLOCK
    assert MOBA_TOPK <= nblk <= MAX_BLOCKS and n_blk_s >= MOBA_TOPK
    row_blk_s = n_rows_p // db

    slopes = 2.0 ** (-8.0 * jnp.arange(1, N_HEADS + 1, dtype=F32) / N_HEADS)
    lb_cum = jnp.cumsum(jax.nn.softmax(hg_lower.astype(F32), axis=0), axis=0)
    lower = lb_cum - lb_cum[:1]

    lane = np.arange(ATT_WIDTH)
    gmat = jnp.asarray((lane[:, None] // HEAD_DIM) == (lane[None, :] // HEAD_DIM), BF16)
    head_mat = jnp.asarray((lane[:, None] // HEAD_DIM) == np.arange(LANES)[None, :], BF16)
    tri, upper = _chunk_masks(MOBA_BLOCK)
    cache_kt = cache_k.transpose(0, 1, 3, 4, 2)
    cache_vt = cache_v.transpose(0, 1, 3, 4, 2)
    cache_kt4 = cache_kt.reshape(depth, n_pool, ATT_WIDTH, PAGE_SIZE)

    x = jnp.concatenate([x_prompt.reshape(n_rows_p, D_MODEL), x_sample.reshape(db, D_MODEL)], axis=0)
    outs = {name: [] for name in ("kp", "vp", "sp", "ks", "vs", "ss")}
    for l in range(depth):
        z = _proj(x, attn_norm[l], w_in[l].astype(BF16))
        lb = lower[l].reshape(1, HG_WIDTH)
        g_out = hg_norm[l].reshape(1, HG_HEAD_DIM)
        qn = jnp.tile(q_norm[l], N_HEADS).reshape(1, ATT_WIDTH)
        kn = jnp.tile(k_norm[l], N_HEADS).reshape(1, ATT_WIDTH)
        wa, wb, wo = (w[l].astype(BF16) for w in (w_branch_a, w_branch_b, w_out))

        oa_p, s_p = _hgrn_prompt(z, lb, g_out, tri, upper, nb, seq)
        k_p, v_p, qt, kh, vt, km = _qk_prompt(z, qn, kn, gmat, nb, nblk)
        kmh = km.reshape(nb, nblk, N_HEADS, HEAD_DIM).transpose(0, 2, 1, 3)
        kmh = jnp.pad(kmh, ((0, 0), (0, 0), (0, MAX_BLOCKS - nblk), (0, 0)))
        ob_p = _moba_prompt(slopes, qt, kh, vt, kmh).reshape(nb, ATT_WIDTH, seq)

        oa_s, s_s = _hgrn_sample(z, lb, g_out, state_hgrn, l, row_blk_s, db)
        q_s, k_s, v_s = _qk_sample(z, qn, kn, gmat, row_blk_s, db)
        km_s = _cache_mean(page_table, cache_kt4, l, db, n_blk_s)
        sel = _sample_select(q_s, km_s, head_mat)[:, :MOBA_TOPK, :N_HEADS]
        sel_blk = sel.transpose(0, 2, 1)
        lpage = sel_blk[..., None] * PAGES_PER_BLOCK + jnp.arange(PAGES_PER_BLOCK)
        phys = jnp.take_along_axis(page_table, lpage.reshape(db, -1), axis=1)
        ob_s = _moba_sample(phys, sel_blk.reshape(db, -1), slopes, q_s, k_s, v_s, cache_kt, cache_vt,
                            l, past_len)

        x = _merge(x, z, oa_p, ob_p, wa, wb, wo, tile=MOBA_BLOCK, row_blk0=0, n_tiles=nb * nblk,
                   ob_transposed=True, tiles_per_seq=nblk)
        x = _merge(x, z, oa_s, ob_s, wa, wb, wo, tile=db, row_blk0=row_blk_s, n_tiles=1,
                   ob_transposed=False)

        i = l // 2
        if l % 2 == 0:
            x = _ffn_dense(x, ffn_norm[l], w_dense_gate[i].astype(BF16), w_dense_up[i].astype(BF16),
                           w_dense_down[i].astype(BF16))
        else:
            x = _moe(x, ffn_norm[l], w_router[i], w_exp_gate[i].astype(BF16), w_exp_up[i].astype(BF16),
                     w_exp_down[i].astype(BF16))

        outs["kp"].append(k_p)
        outs["vp"].append(v_p)
        outs["sp"].append(s_p)
        outs["ks"].append(k_s.reshape(db, 1, N_HEADS, HEAD_DIM))
        outs["vs"].append(v_s.reshape(db, 1, N_HEADS, HEAD_DIM))
        outs["ss"].append(s_s)

    seq_major = lambda rows: jnp.stack(rows).transpose(0, 1, 4, 2, 3)
    return (x[:n_rows_p].reshape(nb, seq, D_MODEL), x[n_rows_p:].reshape(db, 1, D_MODEL),
            seq_major(outs["kp"]), seq_major(outs["vp"]), jnp.stack(outs["sp"]),
            jnp.stack(outs["ks"]), jnp.stack(outs["vs"]), jnp.stack(outs["ss"]))
```

```python
import functools

import numpy as np
import jax
import jax.numpy as jnp
from jax import lax
from jax.experimental import pallas as pl
from jax.experimental.pallas import tpu as pltpu

F32 = jnp.float32
BF16 = jnp.bfloat16

D_MODEL = 1024
HG_WIDTH = 512
HG_HEAD_DIM = 128
HG_HEADS = HG_WIDTH // HG_HEAD_DIM
HG_CHUNK = 16
HEAD_DIM = 64
N_HEADS = 8
ATT_WIDTH = N_HEADS * HEAD_DIM
MOBA_BLOCK = 256
MOBA_TOPK = 3
PAGE_SIZE = 128
PAGES_PER_BLOCK = MOBA_BLOCK // PAGE_SIZE
D_FF = 2816
N_EXPERTS = 8
TOP_K = 2
RMS_EPS = 1e-6
IN_WIDTH = 4 * HG_WIDTH + 3 * ATT_WIDTH + 2 * D_MODEL
COL_ZQ, COL_ZF, COL_ZI, COL_ZG, COL_AQ, COL_AK, COL_AV = range(7)
COL_GA, COL_GB = 7, 9

KEY_WIDTH = 2 * HEAD_DIM
MAX_BLOCKS = 32
FEAT_OFF = MAX_BLOCKS + 3
FEAT_BLK = MAX_BLOCKS + 6
FEAT_END = MAX_BLOCKS + 12
LOG2E = 1.4426950408889634
VAL_ROWS = HEAD_DIM + 16

MASKED = -1e30
VMEM_LIMIT = 56 * 1024 * 1024
FF_TILE = D_FF // 2
MOE_ROWS = 512
DMA_UNROLL = 8
PAGE_GROUP = 16
LANES = 128


def _cparams(sem):
    return pltpu.CompilerParams(dimension_semantics=sem, vmem_limit_bytes=VMEM_LIMIT)


def _row_tile(n, hi):
    for t in range(hi, hi // 4, -16):
        if n % t == 0:
            return t
    return hi


def _split3(a):
    a1 = a.astype(BF16)
    r = a - a1.astype(F32)
    a2 = r.astype(BF16)
    a3 = (r - a2.astype(F32)).astype(BF16)
    return a1, a2, a3


def _dot(a, b):
    return jnp.dot(a, b, preferred_element_type=F32)


def _dot_nt(a, b):
    return lax.dot_general(a, b, (((1,), (1,)), ((), ())), preferred_element_type=F32)


def _dot_f32_lhs(a, b_exact):
    a1, a2, a3 = _split3(a)
    return _dot(a1, b_exact) + _dot(a2, b_exact) + _dot(a3, b_exact)


def _dot_f32_rhs(a_exact, b):
    b1, b2, b3 = _split3(b)
    return _dot(a_exact, b1) + _dot(a_exact, b2) + _dot(a_exact, b3)


def _rms(x, g):
    return x * lax.rsqrt(jnp.mean(x * x, axis=-1, keepdims=True) + RMS_EPS) * g


def _silu(x):
    return x / (1.0 + jnp.exp(-x))


def _sigmoid(x):
    return 1.0 / (1.0 + jnp.exp(-x))


def _top_mask_axis0(g, k):
    idx = lax.broadcasted_iota(jnp.int32, g.shape, 0)
    sel = jnp.zeros(g.shape, jnp.bool_)
    picks = []
    for _ in range(k):
        mx = jnp.max(g, axis=0, keepdims=True)
        first = jnp.min(jnp.where(g == mx, idx, g.shape[0]), axis=0, keepdims=True)
        hit = (idx == first) & (mx > -jnp.inf)
        sel = sel | hit
        g = jnp.where(idx == first, -jnp.inf, g)
        picks.append(first)
    return sel, picks


def _proj_kernel(x_ref, g_ref, w_ref, o_ref, h_scr):
    @pl.when(pl.program_id(1) == 0)
    def _():
        h_scr[...] = _rms(x_ref[...], g_ref[...]).astype(BF16)

    o_ref[...] = _dot(h_scr[...], w_ref[...])


def _proj(x, g, w_bf16):
    n = x.shape[0]
    tm = _row_tile(n, 1024)
    tn = IN_WIDTH // 4
    return pl.pallas_call(
        _proj_kernel,
        grid=(pl.cdiv(n, tm), IN_WIDTH // tn),
        in_specs=[pl.BlockSpec((tm, D_MODEL), lambda i, j: (i, 0)),
                  pl.BlockSpec((1, D_MODEL), lambda i, j: (0, 0)),
                  pl.BlockSpec((D_MODEL, tn), lambda i, j: (0, j))],
        out_specs=pl.BlockSpec((tm, tn), lambda i, j: (i, j)),
        out_shape=jax.ShapeDtypeStruct((n, IN_WIDTH), F32),
        scratch_shapes=[pltpu.VMEM((tm, D_MODEL), BF16)],
        compiler_params=_cparams(("parallel", "arbitrary")),
        name="in_proj",
    )(x, g.reshape(1, D_MODEL), w_bf16)


def _group_rms(x, gain, gmat):
    ms = _dot_f32_lhs(x * x, gmat) * (1.0 / HEAD_DIM)
    return x * lax.rsqrt(ms + RMS_EPS) * gain


def _qk_prompt_kernel(aq_ref, ak_ref, av_ref, qn_ref, kn_ref, gm_ref,
                      kt_out, vt_out, qt_out, kh_out, vtb_out, km_out):
    j = pl.program_id(1)
    gmat = gm_ref[...]
    q = _group_rms(aq_ref[...], qn_ref[...], gmat)
    k = _group_rms(ak_ref[...], kn_ref[...], gmat)
    v = av_ref[...]
    vt = v.T.reshape(N_HEADS, HEAD_DIM, MOBA_BLOCK)
    qt_out[0] = q.T.reshape(N_HEADS, HEAD_DIM, MOBA_BLOCK)
    kt_out[0] = k.T.reshape(N_HEADS, HEAD_DIM, MOBA_BLOCK)
    vt_out[0] = vt
    ones = jnp.ones((N_HEADS, VAL_ROWS - HEAD_DIM, MOBA_BLOCK), BF16)
    vtb_out[0, :, 0] = jnp.concatenate([vt.astype(BF16), ones], axis=1)
    lane = lax.broadcasted_iota(jnp.int32, (MOBA_BLOCK, HEAD_DIM), 1)
    off = lax.broadcasted_iota(jnp.int32, (MOBA_BLOCK, HEAD_DIM), 0).astype(F32)
    feat = jnp.where(lane < FEAT_OFF, off,
                     jnp.where(lane < FEAT_BLK, j.astype(F32), jnp.where(lane < FEAT_END, 1.0, 0.0)))
    ext = jnp.where(lane < MAX_BLOCKS, (lane == j).astype(F32), feat).astype(BF16)
    kb = k.astype(BF16)
    for h in range(N_HEADS):
        kh_out[0, h, 0] = jnp.concatenate([kb[:, h * HEAD_DIM:(h + 1) * HEAD_DIM], ext], axis=1)
    km_out[0, 0] = jnp.sum(k, axis=0, keepdims=True) * (1.0 / MOBA_BLOCK)


def _qk_prompt(z, qn, kn, gmat, nb, nblk):
    seq = nblk * MOBA_BLOCK
    zspec = lambda c: pl.BlockSpec((MOBA_BLOCK, ATT_WIDTH), lambda b, i, c=c: (b * nblk + i, c))
    vec = pl.BlockSpec((1, ATT_WIDTH), lambda b, i: (0, 0))
    head5 = lambda r, c: pl.BlockSpec((1, N_HEADS, 1, r, c), lambda b, i: (b, 0, i, 0, 0))
    tspec = pl.BlockSpec((1, N_HEADS, HEAD_DIM, MOBA_BLOCK), lambda b, i: (b, 0, 0, i))
    tshape = jax.ShapeDtypeStruct((nb, N_HEADS, HEAD_DIM, seq), F32)
    return pl.pallas_call(
        _qk_prompt_kernel,
        grid=(nb, nblk),
        in_specs=[zspec(COL_AQ), zspec(COL_AK), zspec(COL_AV), vec, vec,
                  pl.BlockSpec((ATT_WIDTH, ATT_WIDTH), lambda b, i: (0, 0))],
        out_specs=[tspec, tspec, tspec,
                   head5(MOBA_BLOCK, KEY_WIDTH), head5(VAL_ROWS, MOBA_BLOCK),
                   pl.BlockSpec((1, 1, 1, ATT_WIDTH), lambda b, i: (b, i, 0, 0))],
        out_shape=[tshape, tshape, tshape,
                   jax.ShapeDtypeStruct((nb, N_HEADS, nblk, MOBA_BLOCK, KEY_WIDTH), BF16),
                   jax.ShapeDtypeStruct((nb, N_HEADS, nblk, VAL_ROWS, MOBA_BLOCK), BF16),
                   jax.ShapeDtypeStruct((nb, nblk, 1, ATT_WIDTH), F32)],
        compiler_params=_cparams(("parallel", "parallel")),
        name="qk_prompt",
    )(z, z, z, qn, kn, gmat)


def _qk_sample_kernel(aq_ref, ak_ref, av_ref, qn_ref, kn_ref, gm_ref, q_out, k_out, v_out):
    gmat = gm_ref[...]
    q_out[...] = _group_rms(aq_ref[...], qn_ref[...], gmat)
    k_out[...] = _group_rms(ak_ref[...], kn_ref[...], gmat)
    v_out[...] = av_ref[...]


def _qk_sample(z, qn, kn, gmat, row_blk, db):
    zspec = lambda c: pl.BlockSpec((db, ATT_WIDTH), lambda i, c=c: (row_blk, c))
    vec = pl.BlockSpec((1, ATT_WIDTH), lambda i: (0, 0))
    out = pl.BlockSpec((db, ATT_WIDTH), lambda i: (0, 0))
    shp = jax.ShapeDtypeStruct((db, ATT_WIDTH), F32)
    return pl.pallas_call(
        _qk_sample_kernel,
        grid=(1,),
        in_specs=[zspec(COL_AQ), zspec(COL_AK), zspec(COL_AV), vec, vec,
                  pl.BlockSpec((ATT_WIDTH, ATT_WIDTH), lambda i: (0, 0))],
        out_specs=[out, out, out],
        out_shape=[shp, shp, shp],
        compiler_params=_cparams(("arbitrary",)),
        name="qk_sample",
    )(z, z, z, qn, kn, gmat)


def _moba_prompt_kernel(slopes_ref, qt_ref, kh_ref, vt_ref, km_ref, o_ref,
                        qa_scr, s_scr, p_scr, m_scr, acc_scr):
    i = pl.program_id(1)
    heads = range(N_HEADS)
    n_feat_rows = KEY_WIDTH - HEAD_DIM - MAX_BLOCKS
    qry_pos = lax.broadcasted_iota(jnp.int32, (1, MOBA_BLOCK), 1).astype(F32)
    feat_row = lax.broadcasted_iota(jnp.int32, (n_feat_rows, MOBA_BLOCK), 0)
    blk = lax.broadcasted_iota(jnp.int32, (MAX_BLOCKS, MOBA_BLOCK), 0)
    causal = (lax.broadcasted_iota(jnp.int32, (MOBA_BLOCK, MOBA_BLOCK), 0)
              <= lax.broadcasted_iota(jnp.int32, (MOBA_BLOCK, MOBA_BLOCK), 1))
    ones = jnp.ones((1, MOBA_BLOCK), F32)

    gates = [_dot(km_ref[0, h].astype(BF16), qt_ref[0, h].astype(BF16)) for h in heads]
    for h in heads:
        qt = qt_ref[0, h]
        sel, _ = _top_mask_axis0(jnp.where(blk < i, gates[h], -jnp.inf), MOBA_TOPK)
        sel_bias = jnp.where(sel | (blk >= i), 0.0, MASKED)
        c = slopes_ref[h] * LOG2E
        coefs = (c * ones, (c * MOBA_BLOCK) * ones, -c * qry_pos,
                 (-(c * MOBA_BLOCK) * i.astype(F32)) * ones)
        feat = jnp.zeros((n_feat_rows, MOBA_BLOCK), F32)
        for r, term in enumerate(t for v in coefs for t in _split3(v)):
            feat = jnp.where(feat_row == r, term.astype(F32), feat)
        q_aug = jnp.concatenate([(qt * (HEAD_DIM ** -0.5 * LOG2E)).astype(BF16), sel_bias.astype(BF16),
                                 feat.astype(BF16)], axis=0)
        qa_scr[h] = q_aug

    for h in heads:
        s_scr[h] = jnp.where(causal, _dot(kh_ref[0, h, i], qa_scr[h]), MASKED)
    for h in heads:
        s = s_scr[h]
        m = jnp.max(s, axis=0, keepdims=True)
        p = jnp.exp2(s - m)
        p_scr[h] = p.astype(BF16)
        m_scr[h] = m
    for h in heads:
        acc_scr[h] = _dot(vt_ref[0, h, i], p_scr[h])

    def body(j, carry):
        for h in heads:
            s_scr[h] = _dot(kh_ref[0, h, j], qa_scr[h])
        m_old = [m_scr[h] for h in heads]
        m_new, alpha = [], []
        for h in heads:
            s = s_scr[h]
            m_new.append(jnp.maximum(m_old[h], jnp.max(s, axis=0, keepdims=True)))
            alpha.append(jnp.exp2(m_old[h] - m_new[h]))
            p_scr[h] = jnp.exp2(s - m_new[h]).astype(BF16)
        for h in heads:
            m_scr[h] = m_new[h]
        acc_old = [acc_scr[h] for h in heads]
        pv = [_dot(vt_ref[0, h, j], p_scr[h]) for h in heads]
        for h in heads:
            acc_scr[h] = alpha[h] * acc_old[h] + pv[h]
        return carry

    lax.fori_loop(0, i, body, 0)
    for h in heads:
        acc = acc_scr[h]
        o_ref[0, h] = acc[:HEAD_DIM] / acc[HEAD_DIM:HEAD_DIM + 1]


def _moba_prompt(slopes, qt, kh, vt, kmh):
    nb, _, nblk = kh.shape[:3]
    resident = lambda shape: pl.BlockSpec(shape, lambda b, i, s: (b, 0, 0, 0, 0), pipeline_mode=pl.Buffered(1))
    return pl.pallas_call(
        _moba_prompt_kernel,
        grid_spec=pltpu.PrefetchScalarGridSpec(
            num_scalar_prefetch=1,
            grid=(nb, nblk),
            in_specs=[pl.BlockSpec((1, N_HEADS, HEAD_DIM, MOBA_BLOCK), lambda b, i, s: (b, 0, 0, i)),
                      resident((1, N_HEADS, nblk, MOBA_BLOCK, KEY_WIDTH)),
                      resident((1, N_HEADS, nblk, VAL_ROWS, MOBA_BLOCK)),
                      pl.BlockSpec((1, N_HEADS, MAX_BLOCKS, HEAD_DIM), lambda b, i, s: (b, 0, 0, 0))],
            out_specs=pl.BlockSpec((1, N_HEADS, HEAD_DIM, MOBA_BLOCK), lambda b, i, s: (b, 0, 0, i)),
            scratch_shapes=[pltpu.VMEM((N_HEADS, KEY_WIDTH, MOBA_BLOCK), BF16),
                            pltpu.VMEM((N_HEADS, MOBA_BLOCK, MOBA_BLOCK), F32),
                            pltpu.VMEM((N_HEADS, MOBA_BLOCK, MOBA_BLOCK), BF16),
                            pltpu.VMEM((N_HEADS, 1, MOBA_BLOCK), F32),
                            pltpu.VMEM((N_HEADS, VAL_ROWS, MOBA_BLOCK), F32)]),
        out_shape=jax.ShapeDtypeStruct((nb, N_HEADS, HEAD_DIM, nblk * MOBA_BLOCK), F32),
        compiler_params=_cparams(("parallel", "arbitrary")),
        name="moba_prompt",
    )(slopes, qt, kh, vt, kmh)


def _hgrn_gates(zf, lb):
    log_sig = jnp.minimum(zf, 0.0) - jnp.log(1.0 + jnp.exp(-jnp.abs(zf)))
    a = jnp.log(lb)
    b = jnp.log(1.0 - lb) + log_sig
    logf = jnp.maximum(a, b) + jnp.log(1.0 + jnp.exp(-jnp.abs(a - b)))
    k = (1.0 - lb) / (1.0 + jnp.exp(zf))
    return logf, k


def _hgrn_prompt_kernel(zq_ref, zf_ref, zi_ref, zg_ref, lb_ref, go_ref, tri_ref, upper_ref,
                        o_ref, s_ref, st_scr, oi_scr):
    i = pl.program_id(1)
    tb = zq_ref.shape[0]

    @pl.when(i == 0)
    def _():
        st_scr[...] = jnp.zeros_like(st_scr)

    tri = tri_ref[...]
    upper = upper_ref[...]
    head_cols = [slice(h * HG_HEAD_DIM, (h + 1) * HG_HEAD_DIM) for h in range(HG_HEADS)]

    heads = []
    for cols in head_cols:
        v = zi_ref[:, cols]
        logf, k = _hgrn_gates(zf_ref[:, cols], lb_ref[:, cols])
        b = _dot_f32_rhs(tri, logf)
        rest = _dot_f32_rhs(upper, logf)
        q_in = (zq_ref[:, cols] * jnp.exp(b)).astype(BF16)
        k_in = (k * jnp.exp(-b)).astype(BF16)
        k_out = (k * jnp.exp(rest)).astype(BF16)
        dec = jnp.exp(b + rest)
        vb = v.astype(BF16)
        att = jnp.where(tri > 0, _dot_nt(q_in, k_in), 0.0).astype(BF16)
        o_intra = _dot(att, vb)
        heads.append((q_in, k_out, dec, vb, o_intra))

    sts = [st_scr[h] for h in range(HG_HEADS)]
    for n in range(tb // HG_CHUNK):
        rows = slice(n * HG_CHUNK, (n + 1) * HG_CHUNK)
        for h, cols in enumerate(head_cols):
            q_in, k_out, dec, vb, _ = heads[h]
            oi_scr[rows, cols] = _dot_nt(q_in[rows], sts[h].astype(BF16))
            update = lax.dot_general(vb[rows], k_out[rows], (((0,), (0,)), ((), ())),
                                     preferred_element_type=F32)
            sts[h] = sts[h] * dec[n * HG_CHUNK:n * HG_CHUNK + 1, :] + update

    go = go_ref[...]
    for h, cols in enumerate(head_cols):
        st_scr[h] = sts[h]
        o = heads[h][4] + oi_scr[:, cols]
        o_ref[:, cols] = _rms(o, go) * _silu(zg_ref[:, cols])

    @pl.when(i == pl.num_programs(1) - 1)
    def _():
        for h in range(HG_HEADS):
            s_ref[0, h] = sts[h].T


def _hgrn_prompt(z, lb, g_out, tri, upper, nb, seq):
    tb = MOBA_BLOCK
    nt = seq // tb
    zspec = lambda c: pl.BlockSpec((tb, HG_WIDTH), lambda b, i, c=c: (b * nt + i, c))
    sq = pl.BlockSpec((tb, tb), lambda b, i: (0, 0))
    return pl.pallas_call(
        _hgrn_prompt_kernel,
        grid=(nb, nt),
        in_specs=[zspec(COL_ZQ), zspec(COL_ZF), zspec(COL_ZI), zspec(COL_ZG),
                  pl.BlockSpec((1, HG_WIDTH), lambda b, i: (0, 0)),
                  pl.BlockSpec((1, HG_HEAD_DIM), lambda b, i: (0, 0)),
                  sq, sq],
        out_specs=[pl.BlockSpec((tb, HG_WIDTH), lambda b, i: (b * nt + i, 0)),
                   pl.BlockSpec((1, HG_HEADS, HG_HEAD_DIM, HG_HEAD_DIM), lambda b, i: (b, 0, 0, 0))],
        out_shape=[jax.ShapeDtypeStruct((nb * seq, HG_WIDTH), F32),
                   jax.ShapeDtypeStruct((nb, HG_HEADS, HG_HEAD_DIM, HG_HEAD_DIM), F32)],
        scratch_shapes=[pltpu.VMEM((HG_HEADS, HG_HEAD_DIM, HG_HEAD_DIM), F32),
                        pltpu.VMEM((tb, HG_WIDTH), F32)],
        compiler_params=_cparams(("parallel", "arbitrary")),
        name="hgrn_prompt",
    )(z, z, z, z, lb, g_out, tri, upper)


def _hgrn_sample_kernel(zq_ref, zf_ref, zi_ref, zg_ref, lb_ref, go_ref, s_ref, o_ref, sn_ref):
    db = zq_ref.shape[0]
    eye = (lax.broadcasted_iota(jnp.int32, (HG_HEAD_DIM, HG_HEAD_DIM), 0)
           == lax.broadcasted_iota(jnp.int32, (HG_HEAD_DIM, HG_HEAD_DIM), 1))
    col = lambda r: jnp.sum(jnp.where(eye, r, 0.0), axis=1, keepdims=True)
    lb = lb_ref[...]
    go = go_ref[...]

    def body(b, carry):
        row = pl.ds(b, 1)
        logf, k = _hgrn_gates(zf_ref[row, :], lb)
        s_new = s_ref[0, b, 0] * col(jnp.exp(logf)) + col(k) * zi_ref[row, :]
        sn_ref[b, 0] = s_new
        o = jnp.sum(s_new * col(zq_ref[row, :]), axis=0, keepdims=True)
        o_ref[row, :] = _rms(o, go) * _silu(zg_ref[row, :])
        return carry

    lax.fori_loop(0, db, body, 0)


def _hgrn_sample(z, lb, g_out, state, layer, row_blk, db):
    zspec = lambda c: pl.BlockSpec((db, HG_HEAD_DIM), lambda h, c=c: (row_blk, 4 * c + h))
    return pl.pallas_call(
        _hgrn_sample_kernel,
        grid=(HG_HEADS,),
        in_specs=[zspec(COL_ZQ), zspec(COL_ZF), zspec(COL_ZI), zspec(COL_ZG),
                  pl.BlockSpec((1, HG_HEAD_DIM), lambda h: (0, h)),
                  pl.BlockSpec((1, HG_HEAD_DIM), lambda h: (0, 0)),
                  pl.BlockSpec((1, db, 1, HG_HEAD_DIM, HG_HEAD_DIM), lambda h: (layer, 0, h, 0, 0))],
        out_specs=[pl.BlockSpec((db, HG_HEAD_DIM), lambda h: (0, h)),
                   pl.BlockSpec((db, 1, HG_HEAD_DIM, HG_HEAD_DIM), lambda h: (0, h, 0, 0))],
        out_shape=[jax.ShapeDtypeStruct((db, HG_WIDTH), F32),
                   jax.ShapeDtypeStruct((db, HG_HEADS, HG_HEAD_DIM, HG_HEAD_DIM), F32)],
        compiler_params=_cparams(("parallel",)),
        name="hgrn_sample",
    )(z, z, z, z, lb, g_out, state)


def _cache_mean_kernel(pt_ref, *refs):
    pages, o_ref = refs[:-1], refs[-1]
    g = pl.program_id(1)
    per_step = len(pages) // PAGES_PER_BLOCK
    ones = jnp.ones((8, PAGE_SIZE), BF16)
    for c in range(per_step):
        tot = pages[PAGES_PER_BLOCK * c][0, 0]
        for p in range(1, PAGES_PER_BLOCK):
            tot = tot + pages[PAGES_PER_BLOCK * c + p][0, 0]
        row = sum(_dot_nt(ones, t) for t in _split3(tot))[0:1]
        o_ref[0, pl.ds(g * per_step + c, 1), :] = row * (1.0 / MOBA_BLOCK)


def _cache_mean(page_table, cache_t, layer, db, n_blk):
    n_pages = n_blk * PAGES_PER_BLOCK
    pg = PAGE_GROUP if n_pages % PAGE_GROUP == 0 else PAGES_PER_BLOCK
    spec = lambda k: pl.BlockSpec((1, 1, ATT_WIDTH, PAGE_SIZE),
                                  lambda b, g, pt, k=k: (layer, pt[b, g * pg + k], 0, 0))
    return pl.pallas_call(
        _cache_mean_kernel,
        grid_spec=pltpu.PrefetchScalarGridSpec(
            num_scalar_prefetch=1,
            grid=(db, n_pages // pg),
            in_specs=[spec(k) for k in range(pg)],
            out_specs=pl.BlockSpec((1, n_blk, ATT_WIDTH), lambda b, g, pt: (b, 0, 0))),
        out_shape=jax.ShapeDtypeStruct((db, n_blk, ATT_WIDTH), F32),
        compiler_params=_cparams(("parallel", "arbitrary")),
        name="cache_block_mean",
    )(page_table, *([cache_t] * pg))


def _sample_select_kernel(q_ref, km_ref, hm_ref, o_ref):
    b = pl.program_id(0)
    km = km_ref[0].astype(BF16).astype(F32)
    q = q_ref[pl.ds(b, 1), :].astype(BF16).astype(F32)
    gate = _dot_f32_lhs(km * q, hm_ref[...])
    _, picks = _top_mask_axis0(gate, MOBA_TOPK)
    rows = picks + [jnp.zeros_like(picks[0])] * (o_ref.shape[1] - len(picks))
    o_ref[0] = jnp.concatenate(rows, axis=0)


def _sample_select(q_s, km_s, head_mat):
    db, n_blk, _ = km_s.shape
    return pl.pallas_call(
        _sample_select_kernel,
        grid=(db,),
        in_specs=[pl.BlockSpec((db, ATT_WIDTH), lambda b: (0, 0)),
                  pl.BlockSpec((1, n_blk, ATT_WIDTH), lambda b: (b, 0, 0)),
                  pl.BlockSpec((ATT_WIDTH, LANES), lambda b: (0, 0))],
        out_specs=pl.BlockSpec((1, 8, LANES), lambda b: (b, 0, 0)),
        out_shape=jax.ShapeDtypeStruct((db, 8, LANES), jnp.int32),
        compiler_params=_cparams(("parallel",)),
        name="sample_select",
    )(q_s, km_s, head_mat)


def _moba_sample_kernel(past_len, phys_ref, blk_ref, slopes_ref, q_ref, k_ref, v_ref, *refs):
    pages_per_head = MOBA_TOPK * PAGES_PER_BLOCK
    k_pages, v_pages, o_ref = refs[:pages_per_head], refs[pages_per_head:2 * pages_per_head], refs[-1]
    b = pl.program_id(0)
    h = pl.program_id(1)
    m_keys = MOBA_TOPK * MOBA_BLOCK
    head = pl.ds(h, 1)
    q = q_ref[0, head, :] * (HEAD_DIM ** -0.5)
    kt = jnp.concatenate([r[0, 0, 0] for r in k_pages], axis=1).astype(BF16)
    vt = jnp.concatenate([r[0, 0, 0] for r in v_pages], axis=1).astype(BF16)
    s = _dot(jnp.broadcast_to(q, (8, HEAD_DIM)).astype(BF16), kt)[0:1]
    lane = lax.broadcasted_iota(jnp.int32, (1, m_keys), 1)
    kblk = blk_ref[b, h * MOBA_TOPK + MOBA_TOPK - 1]
    for t in range(MOBA_TOPK - 2, -1, -1):
        kblk = jnp.where(lane < (t + 1) * MOBA_BLOCK, blk_ref[b, h * MOBA_TOPK + t], kblk)
    kpos = kblk * MOBA_BLOCK + lane % MOBA_BLOCK
    s = s - slopes_ref[h] * (past_len - kpos).astype(F32)
    s_self = jnp.sum(q * k_ref[0, head, :], axis=1, keepdims=True)
    m = jnp.maximum(jnp.max(s, axis=1, keepdims=True), s_self)
    p = jnp.exp(s - m)
    p_self = jnp.exp(s_self - m)
    l = jnp.sum(p, axis=1, keepdims=True) + p_self
    o = _dot_nt(jnp.broadcast_to(p, (8, m_keys)).astype(BF16), vt)[0:1] + p_self * v_ref[0, head, :]
    o_ref[0, 0] = o / l


def _moba_sample(phys, sel_blk, slopes, q_s, k_s, v_s, cache_kt, cache_vt, layer, past_len):
    db = q_s.shape[0]
    pages_per_head = MOBA_TOPK * PAGES_PER_BLOCK
    row = pl.BlockSpec((1, N_HEADS, HEAD_DIM), lambda b, h, ph, sb, sl: (b, 0, 0))
    page = lambda t: pl.BlockSpec((1, 1, 1, HEAD_DIM, PAGE_SIZE),
                                  lambda b, h, ph, sb, sl, t=t: (layer, ph[b, h * pages_per_head + t], h, 0, 0))
    pages = [page(t) for t in range(pages_per_head)]
    r3 = lambda a: a.reshape(db, N_HEADS, HEAD_DIM)
    out = pl.pallas_call(
        functools.partial(_moba_sample_kernel, past_len),
        grid_spec=pltpu.PrefetchScalarGridSpec(
            num_scalar_prefetch=3,
            grid=(db, N_HEADS),
            in_specs=[row, row, row] + pages + pages,
            out_specs=pl.BlockSpec((1, 1, 1, HEAD_DIM), lambda b, h, ph, sb, sl: (b, h, 0, 0))),
        out_shape=jax.ShapeDtypeStruct((db, N_HEADS, 1, HEAD_DIM), F32),
        compiler_params=_cparams(("parallel", "arbitrary")),
        name="moba_sample",
    )(phys, sel_blk, slopes, r3(q_s), r3(k_s), r3(v_s),
      *([cache_kt] * pages_per_head), *([cache_vt] * pages_per_head))
    return out.reshape(db, ATT_WIDTH)


def _merge_kernel(ob_transposed, x_ref, oa_ref, ob_ref, ga0, ga1, gb0, gb1, wa_ref, wb_ref, wo_ref, o_ref):
    oa = oa_ref[...].astype(BF16)
    ob = ob_ref[0].T if ob_transposed else ob_ref[...]
    a = _dot(oa, wa_ref[...])
    bm = _dot(ob.astype(BF16), wb_ref[...])
    half = D_MODEL // 2
    m0 = _sigmoid(ga0[...]) * a[:, :half] + _sigmoid(gb0[...]) * bm[:, :half]
    m1 = _sigmoid(ga1[...]) * a[:, half:] + _sigmoid(gb1[...]) * bm[:, half:]
    y = _dot(m0.astype(BF16), wo_ref[:half, :]) + _dot(m1.astype(BF16), wo_ref[half:, :])
    o_ref[...] = x_ref[...] + y


def _merge(x, z, o_a, o_b, wa, wb, wo, *, tile, row_blk0, n_tiles, ob_transposed, tiles_per_seq=None):
    rspec = lambda w, c: pl.BlockSpec((tile, w), lambda i, c=c: (row_blk0 + i, c))
    if ob_transposed:
        ob_spec = pl.BlockSpec((1, ATT_WIDTH, tile), lambda i: (i // tiles_per_seq, 0, i % tiles_per_seq))
    else:
        ob_spec = pl.BlockSpec((tile, ATT_WIDTH), lambda i: (i, 0))
    wspec = lambda r: pl.BlockSpec((r, D_MODEL), lambda i: (0, 0))
    return pl.pallas_call(
        functools.partial(_merge_kernel, ob_transposed),
        grid=(n_tiles,),
        in_specs=[rspec(D_MODEL, 0),
                  pl.BlockSpec((tile, HG_WIDTH), lambda i: (i, 0)), ob_spec,
                  rspec(512, COL_GA), rspec(512, COL_GA + 1), rspec(512, COL_GB), rspec(512, COL_GB + 1),
                  wspec(HG_WIDTH), wspec(ATT_WIDTH), wspec(D_MODEL)],
        out_specs=rspec(D_MODEL, 0),
        out_shape=jax.ShapeDtypeStruct(x.shape, F32),
        input_output_aliases={0: 0},
        compiler_params=_cparams(("parallel",)),
        name="merge_t" if ob_transposed else "merge",
    )(x, o_a, o_b, z, z, z, z, wa, wb, wo)


def _ffn_kernel(x_ref, g_ref, wg_ref, wu_ref, wd_ref, o_ref, h_scr, acc_scr):
    j = pl.program_id(1)

    @pl.when(j == 0)
    def _():
        h_scr[...] = _rms(x_ref[...], g_ref[...]).astype(BF16)
        acc_scr[...] = jnp.zeros_like(acc_scr)

    h = h_scr[...]
    a = _silu(_dot(h, wg_ref[...])) * _dot(h, wu_ref[...])
    acc_scr[...] += _dot(a.astype(BF16), wd_ref[...])

    @pl.when(j == pl.num_programs(1) - 1)
    def _():
        o_ref[...] = x_ref[...] + acc_scr[...]


def _ffn_dense(x, g, wg, wu, wd):
    n = x.shape[0]
    tm = _row_tile(n, 512)
    return pl.pallas_call(
        _ffn_kernel,
        grid=(pl.cdiv(n, tm), D_FF // FF_TILE),
        in_specs=[pl.BlockSpec((tm, D_MODEL), lambda i, j: (i, 0)),
                  pl.BlockSpec((1, D_MODEL), lambda i, j: (0, 0)),
                  pl.BlockSpec((D_MODEL, FF_TILE), lambda i, j: (0, j)),
                  pl.BlockSpec((D_MODEL, FF_TILE), lambda i, j: (0, j)),
                  pl.BlockSpec((FF_TILE, D_MODEL), lambda i, j: (j, 0))],
        out_specs=pl.BlockSpec((tm, D_MODEL), lambda i, j: (i, 0)),
        out_shape=jax.ShapeDtypeStruct((n, D_MODEL), F32),
        scratch_shapes=[pltpu.VMEM((tm, D_MODEL), BF16), pltpu.VMEM((tm, D_MODEL), F32)],
        compiler_params=_cparams(("parallel", "arbitrary")),
        name="ffn_dense",
    )(x, g.reshape(1, D_MODEL), wg, wu, wd)


def _router_kernel(x_ref, g_ref, wr_ref, h_ref, logit_ref):
    h = _rms(x_ref[...], g_ref[...])
    h_ref[...] = h
    logit_ref[...] = _dot(h.astype(BF16), wr_ref[...].astype(BF16))


def _router(x, g, wr_pad):
    n = x.shape[0]
    tm = _row_tile(n, 1024)
    return pl.pallas_call(
        _router_kernel,
        grid=(pl.cdiv(n, tm),),
        in_specs=[pl.BlockSpec((tm, D_MODEL), lambda i: (i, 0)),
                  pl.BlockSpec((1, D_MODEL), lambda i: (0, 0)),
                  pl.BlockSpec((D_MODEL, LANES), lambda i: (0, 0))],
        out_specs=[pl.BlockSpec((tm, D_MODEL), lambda i: (i, 0)),
                   pl.BlockSpec((tm, LANES), lambda i: (i, 0))],
        out_shape=[jax.ShapeDtypeStruct((n, D_MODEL), F32),
                   jax.ShapeDtypeStruct((n, LANES), F32)],
        compiler_params=_cparams(("parallel",)),
        name="moe_router",
    )(x, g.reshape(1, D_MODEL), wr_pad)


def _expert_kernel(be_ref, used_ref, x_ref, wg_ref, wu_ref, wd_ref, o_ref, acc_scr):
    i = pl.program_id(0)
    j = pl.program_id(1)

    @pl.when(j == 0)
    def _():
        acc_scr[...] = jnp.zeros_like(acc_scr)

    @pl.when(used_ref[i] > 0)
    def _():
        x = x_ref[...].astype(BF16)
        a = _silu(_dot(x, wg_ref[0])) * _dot(x, wu_ref[0])
        acc_scr[...] += _dot(a.astype(BF16), wd_ref[0])

    @pl.when(j == pl.num_programs(1) - 1)
    def _():
        o_ref[...] = acc_scr[...]


def _experts(blk_e, blk_used, xb, weg, weu, wed):
    rows = xb.shape[0] // MOE_ROWS * MOE_ROWS
    return pl.pallas_call(
        _expert_kernel,
        grid_spec=pltpu.PrefetchScalarGridSpec(
            num_scalar_prefetch=2,
            grid=(rows // MOE_ROWS, D_FF // FF_TILE),
            in_specs=[pl.BlockSpec((MOE_ROWS, D_MODEL), lambda i, j, be, bu: (i, 0)),
                      pl.BlockSpec((1, D_MODEL, FF_TILE), lambda i, j, be, bu: (be[i], 0, j)),
                      pl.BlockSpec((1, D_MODEL, FF_TILE), lambda i, j, be, bu: (be[i], 0, j)),
                      pl.BlockSpec((1, FF_TILE, D_MODEL), lambda i, j, be, bu: (be[i], j, 0))],
            out_specs=pl.BlockSpec((MOE_ROWS, D_MODEL), lambda i, j, be, bu: (i, 0)),
            scratch_shapes=[pltpu.VMEM((MOE_ROWS, D_MODEL), F32)]),
        out_shape=jax.ShapeDtypeStruct((rows, D_MODEL), F32),
        compiler_params=_cparams(("parallel", "arbitrary")),
        name="moe_experts",
    )(blk_e, blk_used, xb, weg, weu, wed)


def _row_copy(src, src_row, dst, dst_row, sem):
    return pltpu.make_async_copy(src.at[pl.ds(src_row, 1)], dst.at[pl.ds(dst_row, 1)], sem)


def _dispatch_kernel(dest_ref, h_ref, xb_in_ref, xb_ref, sem):
    del xb_in_ref
    tile = h_ref.shape[0]

    def start(r, carry):
        for k in range(TOP_K):
            _row_copy(h_ref, r, xb_ref, dest_ref[0, 0, r * TOP_K + k], sem).start()
        return carry

    def wait(r, carry):
        for k in range(TOP_K):
            _row_copy(h_ref, r, xb_ref, dest_ref[0, 0, r * TOP_K + k], sem).wait()
        return carry

    lax.fori_loop(0, tile, start, 0, unroll=DMA_UNROLL)
    lax.fori_loop(0, tile, wait, 0, unroll=DMA_UNROLL)


def _dispatch(dest, h, n_rows):
    n = h.shape[0]
    tile = _row_tile(n, 512)
    n_tiles = pl.cdiv(n, tile)
    dest_p = jnp.pad(dest, ((0, n_tiles * tile - n), (0, 0)), constant_values=n_rows)
    hbm = pl.BlockSpec(memory_space=pl.ANY)
    return pl.pallas_call(
        _dispatch_kernel,
        grid=(n_tiles,),
        in_specs=[pl.BlockSpec((1, 1, tile * TOP_K), lambda i: (i, 0, 0), memory_space=pltpu.SMEM),
                  pl.BlockSpec((tile, D_MODEL), lambda i: (i, 0)), hbm],
        out_specs=hbm,
        out_shape=jax.ShapeDtypeStruct((n_rows + 8, D_MODEL), F32),
        scratch_shapes=[pltpu.SemaphoreType.DMA(())],
        input_output_aliases={2: 0},
        compiler_params=_cparams(("arbitrary",)),
        name="moe_dispatch",
    )(dest_p.reshape(n_tiles, 1, tile * TOP_K), h, jnp.zeros((n_rows + 8, D_MODEL), F32))


def _combine_kernel(slot_ref, x_ref, gate_ref, yb_ref, o_ref, ybuf, sem):
    tile = x_ref.shape[0]

    def start(r, carry):
        for k in range(TOP_K):
            _row_copy(yb_ref, slot_ref[0, 0, r * TOP_K + k], ybuf.at[k], r, sem).start()
        return carry

    def wait(r, carry):
        for k in range(TOP_K):
            _row_copy(yb_ref, slot_ref[0, 0, r * TOP_K + k], ybuf.at[k], r, sem).wait()
        return carry

    lax.fori_loop(0, tile, start, 0, unroll=DMA_UNROLL)
    lax.fori_loop(0, tile, wait, 0, unroll=DMA_UNROLL)
    y = x_ref[...]
    for k in range(TOP_K):
        y = y + gate_ref[:, k:k + 1] * ybuf[k]
    o_ref[...] = y


def _combine(x, gates, dest, yb):
    n = x.shape[0]
    tile = _row_tile(n, 512)
    n_tiles = pl.cdiv(n, tile)
    slot_p = jnp.pad(dest, ((0, n_tiles * tile - n), (0, 0))).reshape(n_tiles, 1, tile * TOP_K)
    return pl.pallas_call(
        _combine_kernel,
        grid=(n_tiles,),
        in_specs=[pl.BlockSpec((1, 1, tile * TOP_K), lambda i: (i, 0, 0), memory_space=pltpu.SMEM),
                  pl.BlockSpec((tile, D_MODEL), lambda i: (i, 0)),
                  pl.BlockSpec((tile, TOP_K), lambda i: (i, 0)),
                  pl.BlockSpec(memory_space=pl.ANY)],
        out_specs=pl.BlockSpec((tile, D_MODEL), lambda i: (i, 0)),
        out_shape=jax.ShapeDtypeStruct((n, D_MODEL), F32),
        scratch_shapes=[pltpu.VMEM((TOP_K, tile, D_MODEL), F32), pltpu.SemaphoreType.DMA(())],
        compiler_params=_cparams(("arbitrary",)),
        name="moe_combine",
    )(slot_p, x, gates, yb)


def _moe(x, g, wr, weg, weu, wed):
    n = x.shape[0]
    wr_pad = jnp.zeros((D_MODEL, LANES), F32).at[:, :N_EXPERTS].set(wr)
    h, logits = _router(x, g, wr_pad)
    top_l, top_e = lax.top_k(logits[:, :N_EXPERTS], TOP_K)
    gates = jax.nn.softmax(top_l, axis=-1)
    n_slots = n * TOP_K
    n_blocks = -(-(n_slots + N_EXPERTS * (MOE_ROWS - 1)) // MOE_ROWS)
    onehot = (top_e.reshape(n_slots, 1) == jnp.arange(N_EXPERTS)).astype(jnp.int32)
    upto = jnp.cumsum(onehot, axis=0)
    counts = upto[-1]
    padded = (counts + MOE_ROWS - 1) // MOE_ROWS * MOE_ROWS
    end_pad = jnp.cumsum(padded)
    start_pad = end_pad - padded
    dest = jnp.sum(onehot * (start_pad + upto - 1), axis=1).astype(jnp.int32).reshape(n, TOP_K)
    blk_start = jnp.arange(n_blocks) * MOE_ROWS
    blk_e = jnp.minimum(jnp.searchsorted(end_pad, blk_start, side='right'), N_EXPERTS - 1).astype(jnp.int32)
    blk_used = (blk_start < end_pad[-1]).astype(jnp.int32)
    xb = _dispatch(dest, h, n_blocks * MOE_ROWS)
    yb = _experts(blk_e, blk_used, xb, weg, weu, wed)
    return _combine(x, gates, dest, yb)


def _chunk_masks(tb):
    t = np.arange(tb)
    same = (t[:, None] // HG_CHUNK) == (t[None, :] // HG_CHUNK)
    tri = same & (t[None, :] <= t[:, None])
    upper = same & (t[None, :] > t[:, None])
    return jnp.asarray(tri, BF16), jnp.asarray(upper, BF16)


def kernel(x_prompt, x_sample, cache_k, cache_v, state_hgrn, page_table, w_in, hg_lower, hg_norm, q_norm, k_norm, w_branch_a, w_branch_b, w_out, attn_norm, ffn_norm, w_dense_gate, w_dense_up, w_dense_down, w_router, w_exp_gate, w_exp_up, w_exp_down):
    nb, seq, _ = x_prompt.shape
    db, dec_seq, _ = x_sample.shape
    depth, n_pool = cache_k.shape[:2]
    n_pages = page_table.shape[1]
    past_len = n_pages * PAGE_SIZE
    assert dec_seq == 1 and seq % MOBA_BLOCK == 0 and past_len % MOBA_BLOCK == 0
    n_rows_p = nb * seq
    assert n_rows_p % db == 0 and db % 8 == 0
    nblk = seq // MOBA_BLOCK
    n_blk_s = past_len // MOBA_BLOCK
    assert MOBA_TOPK <= nblk <= MAX_BLOCKS and n_blk_s >= MOBA_TOPK
    row_blk_s = n_rows_p // db

    slopes = 2.0 ** (-8.0 * jnp.arange(1, N_HEADS + 1, dtype=F32) / N_HEADS)
    lb_cum = jnp.cumsum(jax.nn.softmax(hg_lower.astype(F32), axis=0), axis=0)
    lower = lb_cum - lb_cum[:1]

    lane = np.arange(ATT_WIDTH)
    gmat = jnp.asarray((lane[:, None] // HEAD_DIM) == (lane[None, :] // HEAD_DIM), BF16)
    head_mat = jnp.asarray((lane[:, None] // HEAD_DIM) == np.arange(LANES)[None, :], BF16)
    tri, upper = _chunk_masks(MOBA_BLOCK)
    cache_kt = cache_k.transpose(0, 1, 3, 4, 2)
    cache_vt = cache_v.transpose(0, 1, 3, 4, 2)
    cache_kt4 = cache_kt.reshape(depth, n_pool, ATT_WIDTH, PAGE_SIZE)

    x = jnp.concatenate([x_prompt.reshape(n_rows_p, D_MODEL), x_sample.reshape(db, D_MODEL)], axis=0)
    outs = {name: [] for name in ("kp", "vp", "sp", "ks", "vs", "ss")}
    for l in range(depth):
        z = _proj(x, attn_norm[l], w_in[l].astype(BF16))
        lb = lower[l].reshape(1, HG_WIDTH)
        g_out = hg_norm[l].reshape(1, HG_HEAD_DIM)
        qn = jnp.tile(q_norm[l], N_HEADS).reshape(1, ATT_WIDTH)
        kn = jnp.tile(k_norm[l], N_HEADS).reshape(1, ATT_WIDTH)
        wa, wb, wo = (w[l].astype(BF16) for w in (w_branch_a, w_branch_b, w_out))

        oa_p, s_p = _hgrn_prompt(z, lb, g_out, tri, upper, nb, seq)
        k_p, v_p, qt, kh, vt, km = _qk_prompt(z, qn, kn, gmat, nb, nblk)
        kmh = km.reshape(nb, nblk, N_HEADS, HEAD_DIM).transpose(0, 2, 1, 3)
        kmh = jnp.pad(kmh, ((0, 0), (0, 0), (0, MAX_BLOCKS - nblk), (0, 0)))
        ob_p = _moba_prompt(slopes, qt, kh, vt, kmh).reshape(nb, ATT_WIDTH, seq)

        oa_s, s_s = _hgrn_sample(z, lb, g_out, state_hgrn, l, row_blk_s, db)
        q_s, k_s, v_s = _qk_sample(z, qn, kn, gmat, row_blk_s, db)
        km_s = _cache_mean(page_table, cache_kt4, l, db, n_blk_s)
        sel = _sample_select(q_s, km_s, head_mat)[:, :MOBA_TOPK, :N_HEADS]
        sel_blk = sel.transpose(0, 2, 1)
        lpage = sel_blk[..., None] * PAGES_PER_BLOCK + jnp.arange(PAGES_PER_BLOCK)
        phys = jnp.take_along_axis(page_table, lpage.reshape(db, -1), axis=1)
        ob_s = _moba_sample(phys, sel_blk.reshape(db, -1), slopes, q_s, k_s, v_s, cache_kt, cache_vt,
                            l, past_len)

        mt = 2 * MOBA_BLOCK if seq % (2 * MOBA_BLOCK) == 0 else MOBA_BLOCK
        x = _merge(x, z, oa_p, ob_p, wa, wb, wo, tile=mt, row_blk0=0, n_tiles=n_rows_p // mt,
                   ob_transposed=True, tiles_per_seq=seq // mt)
        x = _merge(x, z, oa_s, ob_s, wa, wb, wo, tile=db, row_blk0=row_blk_s, n_tiles=1,
                   ob_transposed=False)

        i = l // 2
        if l % 2 == 0:
            x = _ffn_dense(x, ffn_norm[l], w_dense_gate[i].astype(BF16), w_dense_up[i].astype(BF16),
                           w_dense_down[i].astype(BF16))
        else:
            x = _moe(x, ffn_norm[l], w_router[i], w_exp_gate[i].astype(BF16), w_exp_up[i].astype(BF16),
                     w_exp_down[i].astype(BF16))

        outs["kp"].append(k_p)
        outs["vp"].append(v_p)
        outs["sp"].append(s_p)
        outs["ks"].append(k_s.reshape(db, 1, N_HEADS, HEAD_DIM))
        outs["vs"].append(v_s.reshape(db, 1, N_HEADS, HEAD_DIM))
        outs["ss"].append(s_s)

    seq_major = lambda rows: jnp.stack(rows).transpose(0, 1, 4, 2, 3)
    return (x[:n_rows_p].reshape(nb, seq, D_MODEL), x[n_rows_p:].reshape(db, 1, D_MODEL),
            seq_major(outs["kp"]), seq_major(outs["vp"]), jnp.stack(outs["sp"]),
            jnp.stack(outs["ks"]), jnp.stack(outs["vs"]), jnp.stack(outs["ss"]))
```

```python
import functools

import numpy as np
import jax
import jax.numpy as jnp
from jax import lax
from jax.experimental import pallas as pl
from jax.experimental.pallas import tpu as pltpu

F32 = jnp.float32
BF16 = jnp.bfloat16

D_MODEL = 1024
HG_WIDTH = 512
HG_HEAD_DIM = 128
HG_HEADS = HG_WIDTH // HG_HEAD_DIM
HG_CHUNK = 16
HEAD_DIM = 64
N_HEADS = 8
ATT_WIDTH = N_HEADS * HEAD_DIM
MOBA_BLOCK = 256
MOBA_TOPK = 3
PAGE_SIZE = 128
PAGES_PER_BLOCK = MOBA_BLOCK // PAGE_SIZE
D_FF = 2816
N_EXPERTS = 8
TOP_K = 2
RMS_EPS = 1e-6
IN_WIDTH = 4 * HG_WIDTH + 3 * ATT_WIDTH + 2 * D_MODEL
COL_ZQ, COL_ZF, COL_ZI, COL_ZG, COL_AQ, COL_AK, COL_AV = range(7)
COL_GA, COL_GB = 7, 9

KEY_WIDTH = 2 * HEAD_DIM
MAX_BLOCKS = 32
FEAT_OFF = MAX_BLOCKS + 3
FEAT_BLK = MAX_BLOCKS + 6
FEAT_END = MAX_BLOCKS + 12
LOG2E = 1.4426950408889634
VAL_ROWS = HEAD_DIM + 16
BLOCKS_PER_PASS = 2

MASKED = -1e30
VMEM_LIMIT = 56 * 1024 * 1024
FF_TILE = D_FF // 2
MOE_ROWS = 512
DMA_UNROLL = 8
PAGE_GROUP = 16
SAMPLE_HEADS_PER_STEP = 2
LANES = 128


def _cparams(sem):
    return pltpu.CompilerParams(dimension_semantics=sem, vmem_limit_bytes=VMEM_LIMIT)


def _row_tile(n, hi):
    for t in range(hi, hi // 4, -16):
        if n % t == 0:
            return t
    return hi


def _split3(a):
    a1 = a.astype(BF16)
    r = a - a1.astype(F32)
    a2 = r.astype(BF16)
    a3 = (r - a2.astype(F32)).astype(BF16)
    return a1, a2, a3


def _dot(a, b):
    return jnp.dot(a, b, preferred_element_type=F32)


def _dot_nt(a, b):
    return lax.dot_general(a, b, (((1,), (1,)), ((), ())), preferred_element_type=F32)


def _dot_f32_lhs(a, b_exact):
    a1, a2, a3 = _split3(a)
    return _dot(a1, b_exact) + _dot(a2, b_exact) + _dot(a3, b_exact)


def _dot_f32_rhs(a_exact, b):
    b1, b2, b3 = _split3(b)
    return _dot(a_exact, b1) + _dot(a_exact, b2) + _dot(a_exact, b3)


def _rms(x, g):
    return x * lax.rsqrt(jnp.mean(x * x, axis=-1, keepdims=True) + RMS_EPS) * g


def _silu(x):
    return x / (1.0 + jnp.exp(-x))


def _sigmoid(x):
    return 1.0 / (1.0 + jnp.exp(-x))


def _top_mask_axis0(g, k):
    idx = lax.broadcasted_iota(jnp.int32, g.shape, 0)
    sel = jnp.zeros(g.shape, jnp.bool_)
    picks = []
    for _ in range(k):
        mx = jnp.max(g, axis=0, keepdims=True)
        first = jnp.min(jnp.where(g == mx, idx, g.shape[0]), axis=0, keepdims=True)
        hit = (idx == first) & (mx > -jnp.inf)
        sel = sel | hit
        g = jnp.where(idx == first, -jnp.inf, g)
        picks.append(first)
    return sel, picks


def _proj_kernel(x_ref, g_ref, w_ref, o_ref, h_scr):
    @pl.when(pl.program_id(1) == 0)
    def _():
        h_scr[...] = _rms(x_ref[...], g_ref[...]).astype(BF16)

    o_ref[...] = _dot(h_scr[...], w_ref[...])


def _proj(x, g, w_bf16):
    n = x.shape[0]
    tm = _row_tile(n, 1024)
    tn = IN_WIDTH // 4
    return pl.pallas_call(
        _proj_kernel,
        grid=(pl.cdiv(n, tm), IN_WIDTH // tn),
        in_specs=[pl.BlockSpec((tm, D_MODEL), lambda i, j: (i, 0)),
                  pl.BlockSpec((1, D_MODEL), lambda i, j: (0, 0)),
                  pl.BlockSpec((D_MODEL, tn), lambda i, j: (0, j))],
        out_specs=pl.BlockSpec((tm, tn), lambda i, j: (i, j)),
        out_shape=jax.ShapeDtypeStruct((n, IN_WIDTH), F32),
        scratch_shapes=[pltpu.VMEM((tm, D_MODEL), BF16)],
        compiler_params=_cparams(("parallel", "arbitrary")),
        name="in_proj",
    )(x, g.reshape(1, D_MODEL), w_bf16)


def _group_rms(x, gain, gmat):
    ms = _dot_f32_lhs(x * x, gmat) * (1.0 / HEAD_DIM)
    return x * lax.rsqrt(ms + RMS_EPS) * gain


def _qk_prompt_kernel(aq_ref, ak_ref, av_ref, qn_ref, kn_ref, gm_ref,
                      kt_out, vt_out, qt_out, kh_out, vtb_out, km_out):
    j = pl.program_id(1)
    gmat = gm_ref[...]
    q = _group_rms(aq_ref[...], qn_ref[...], gmat)
    k = _group_rms(ak_ref[...], kn_ref[...], gmat)
    v = av_ref[...]
    vt = v.T.reshape(N_HEADS, HEAD_DIM, MOBA_BLOCK)
    qt_out[0] = q.T.reshape(N_HEADS, HEAD_DIM, MOBA_BLOCK)
    kt_out[0] = k.T.reshape(N_HEADS, HEAD_DIM, MOBA_BLOCK)
    vt_out[0] = vt
    ones = jnp.ones((N_HEADS, VAL_ROWS - HEAD_DIM, MOBA_BLOCK), BF16)
    vtb_out[0, :, 0] = jnp.concatenate([vt.astype(BF16), ones], axis=1)
    lane = lax.broadcasted_iota(jnp.int32, (MOBA_BLOCK, HEAD_DIM), 1)
    off = lax.broadcasted_iota(jnp.int32, (MOBA_BLOCK, HEAD_DIM), 0).astype(F32)
    feat = jnp.where(lane < FEAT_OFF, off,
                     jnp.where(lane < FEAT_BLK, j.astype(F32), jnp.where(lane < FEAT_END, 1.0, 0.0)))
    ext = jnp.where(lane < MAX_BLOCKS, (lane == j).astype(F32), feat).astype(BF16)
    kb = k.astype(BF16)
    for h in range(N_HEADS):
        kh_out[0, h, 0] = jnp.concatenate([kb[:, h * HEAD_DIM:(h + 1) * HEAD_DIM], ext], axis=1)
    km_out[0, 0] = jnp.sum(k, axis=0, keepdims=True) * (1.0 / MOBA_BLOCK)


def _qk_prompt(z, qn, kn, gmat, nb, nblk):
    seq = nblk * MOBA_BLOCK
    zspec = lambda c: pl.BlockSpec((MOBA_BLOCK, ATT_WIDTH), lambda b, i, c=c: (b * nblk + i, c))
    vec = pl.BlockSpec((1, ATT_WIDTH), lambda b, i: (0, 0))
    head5 = lambda r, c: pl.BlockSpec((1, N_HEADS, 1, r, c), lambda b, i: (b, 0, i, 0, 0))
    tspec = pl.BlockSpec((1, N_HEADS, HEAD_DIM, MOBA_BLOCK), lambda b, i: (b, 0, 0, i))
    tshape = jax.ShapeDtypeStruct((nb, N_HEADS, HEAD_DIM, seq), F32)
    return pl.pallas_call(
        _qk_prompt_kernel,
        grid=(nb, nblk),
        in_specs=[zspec(COL_AQ), zspec(COL_AK), zspec(COL_AV), vec, vec,
                  pl.BlockSpec((ATT_WIDTH, ATT_WIDTH), lambda b, i: (0, 0))],
        out_specs=[tspec, tspec, tspec,
                   head5(MOBA_BLOCK, KEY_WIDTH), head5(VAL_ROWS, MOBA_BLOCK),
                   pl.BlockSpec((1, 1, 1, ATT_WIDTH), lambda b, i: (b, i, 0, 0))],
        out_shape=[tshape, tshape, tshape,
                   jax.ShapeDtypeStruct((nb, N_HEADS, nblk, MOBA_BLOCK, KEY_WIDTH), BF16),
                   jax.ShapeDtypeStruct((nb, N_HEADS, nblk, VAL_ROWS, MOBA_BLOCK), BF16),
                   jax.ShapeDtypeStruct((nb, nblk, 1, ATT_WIDTH), F32)],
        compiler_params=_cparams(("parallel", "parallel")),
        name="qk_prompt",
    )(z, z, z, qn, kn, gmat)


def _qk_sample_kernel(aq_ref, ak_ref, av_ref, qn_ref, kn_ref, gm_ref, q_out, k_out, v_out):
    gmat = gm_ref[...]
    q_out[...] = _group_rms(aq_ref[...], qn_ref[...], gmat)
    k_out[...] = _group_rms(ak_ref[...], kn_ref[...], gmat)
    v_out[...] = av_ref[...]


def _qk_sample(z, qn, kn, gmat, row_blk, db):
    zspec = lambda c: pl.BlockSpec((db, ATT_WIDTH), lambda i, c=c: (row_blk, c))
    vec = pl.BlockSpec((1, ATT_WIDTH), lambda i: (0, 0))
    out = pl.BlockSpec((db, ATT_WIDTH), lambda i: (0, 0))
    shp = jax.ShapeDtypeStruct((db, ATT_WIDTH), F32)
    return pl.pallas_call(
        _qk_sample_kernel,
        grid=(1,),
        in_specs=[zspec(COL_AQ), zspec(COL_AK), zspec(COL_AV), vec, vec,
                  pl.BlockSpec((ATT_WIDTH, ATT_WIDTH), lambda i: (0, 0))],
        out_specs=[out, out, out],
        out_shape=[shp, shp, shp],
        compiler_params=_cparams(("arbitrary",)),
        name="qk_sample",
    )(z, z, z, qn, kn, gmat)


def _moba_prompt_kernel(slopes_ref, qt_ref, kh_ref, vt_ref, km_ref, o_ref,
                        qa_scr, s_scr, p_scr, m_scr, acc_scr):
    i = pl.program_id(1)
    heads = range(N_HEADS)
    n_feat_rows = KEY_WIDTH - HEAD_DIM - MAX_BLOCKS
    qry_pos = lax.broadcasted_iota(jnp.int32, (1, MOBA_BLOCK), 1).astype(F32)
    feat_row = lax.broadcasted_iota(jnp.int32, (n_feat_rows, MOBA_BLOCK), 0)
    blk = lax.broadcasted_iota(jnp.int32, (MAX_BLOCKS, MOBA_BLOCK), 0)
    causal = (lax.broadcasted_iota(jnp.int32, (MOBA_BLOCK, MOBA_BLOCK), 0)
              <= lax.broadcasted_iota(jnp.int32, (MOBA_BLOCK, MOBA_BLOCK), 1))
    ones = jnp.ones((1, MOBA_BLOCK), F32)

    gates = [_dot(km_ref[0, h].astype(BF16), qt_ref[0, h].astype(BF16)) for h in heads]
    for h in heads:
        qt = qt_ref[0, h]
        sel, _ = _top_mask_axis0(jnp.where(blk < i, gates[h], -jnp.inf), MOBA_TOPK)
        sel_bias = jnp.where(sel | (blk >= i), 0.0, MASKED)
        c = slopes_ref[h] * LOG2E
        coefs = (c * ones, (c * MOBA_BLOCK) * ones, -c * qry_pos,
                 (-(c * MOBA_BLOCK) * i.astype(F32)) * ones)
        feat = jnp.zeros((n_feat_rows, MOBA_BLOCK), F32)
        for r, term in enumerate(t for v in coefs for t in _split3(v)):
            feat = jnp.where(feat_row == r, term.astype(F32), feat)
        q_aug = jnp.concatenate([(qt * (HEAD_DIM ** -0.5 * LOG2E)).astype(BF16), sel_bias.astype(BF16),
                                 feat.astype(BF16)], axis=0)
        qa_scr[h] = q_aug

    for h in heads:
        s_scr[h] = jnp.where(causal, _dot(kh_ref[0, h, i], qa_scr[h]), MASKED)
    for h in heads:
        s = s_scr[h]
        m = jnp.max(s, axis=0, keepdims=True)
        p = jnp.exp2(s - m)
        p_scr[h] = p.astype(BF16)
        m_scr[h] = m
    for h in heads:
        acc_scr[h] = _dot(vt_ref[0, h, i], p_scr[h])

    def attend(blocks):
        for u, j in enumerate(blocks):
            for h in heads:
                s_scr[u * N_HEADS + h] = _dot(kh_ref[0, h, j], qa_scr[h])
        m_old = [m_scr[h] for h in heads]
        m_new, alpha = [], []
        for h in heads:
            ss = [s_scr[u * N_HEADS + h] for u in range(len(blocks))]
            top = m_old[h]
            for s in ss:
                top = jnp.maximum(top, jnp.max(s, axis=0, keepdims=True))
            m_new.append(top)
            alpha.append(jnp.exp2(m_old[h] - top))
            for u, s in enumerate(ss):
                p_scr[u * N_HEADS + h] = jnp.exp2(s - top).astype(BF16)
        for h in heads:
            m_scr[h] = m_new[h]
        acc_old = [acc_scr[h] for h in heads]
        pv = []
        for h in heads:
            t = _dot(vt_ref[0, h, blocks[0]], p_scr[h])
            for u in range(1, len(blocks)):
                t = t + _dot(vt_ref[0, h, blocks[u]], p_scr[u * N_HEADS + h])
            pv.append(t)
        for h in heads:
            acc_scr[h] = alpha[h] * acc_old[h] + pv[h]

    def pair(t, carry):
        attend([BLOCKS_PER_PASS * t + u for u in range(BLOCKS_PER_PASS)])
        return carry

    lax.fori_loop(0, i // BLOCKS_PER_PASS, pair, 0)

    def single(j, carry):
        attend([j])
        return carry

    lax.fori_loop(i // BLOCKS_PER_PASS * BLOCKS_PER_PASS, i, single, 0)
    for h in heads:
        acc = acc_scr[h]
        o_ref[0, h] = acc[:HEAD_DIM] / acc[HEAD_DIM:HEAD_DIM + 1]


def _moba_prompt(slopes, qt, kh, vt, kmh):
    nb, _, nblk = kh.shape[:3]
    resident = lambda shape: pl.BlockSpec(shape, lambda b, i, s: (b, 0, 0, 0, 0), pipeline_mode=pl.Buffered(1))
    return pl.pallas_call(
        _moba_prompt_kernel,
        grid_spec=pltpu.PrefetchScalarGridSpec(
            num_scalar_prefetch=1,
            grid=(nb, nblk),
            in_specs=[pl.BlockSpec((1, N_HEADS, HEAD_DIM, MOBA_BLOCK), lambda b, i, s: (b, 0, 0, i)),
                      resident((1, N_HEADS, nblk, MOBA_BLOCK, KEY_WIDTH)),
                      resident((1, N_HEADS, nblk, VAL_ROWS, MOBA_BLOCK)),
                      pl.BlockSpec((1, N_HEADS, MAX_BLOCKS, HEAD_DIM), lambda b, i, s: (b, 0, 0, 0))],
            out_specs=pl.BlockSpec((1, N_HEADS, HEAD_DIM, MOBA_BLOCK), lambda b, i, s: (b, 0, 0, i)),
            scratch_shapes=[pltpu.VMEM((N_HEADS, KEY_WIDTH, MOBA_BLOCK), BF16),
                            pltpu.VMEM((BLOCKS_PER_PASS * N_HEADS, MOBA_BLOCK, MOBA_BLOCK), F32),
                            pltpu.VMEM((BLOCKS_PER_PASS * N_HEADS, MOBA_BLOCK, MOBA_BLOCK), BF16),
                            pltpu.VMEM((N_HEADS, 1, MOBA_BLOCK), F32),
                            pltpu.VMEM((N_HEADS, VAL_ROWS, MOBA_BLOCK), F32)]),
        out_shape=jax.ShapeDtypeStruct((nb, N_HEADS, HEAD_DIM, nblk * MOBA_BLOCK), F32),
        compiler_params=_cparams(("parallel", "arbitrary")),
        name="moba_prompt",
    )(slopes, qt, kh, vt, kmh)


def _hgrn_gates(zf, lb):
    log_sig = jnp.minimum(zf, 0.0) - jnp.log(1.0 + jnp.exp(-jnp.abs(zf)))
    a = jnp.log(lb)
    b = jnp.log(1.0 - lb) + log_sig
    logf = jnp.maximum(a, b) + jnp.log(1.0 + jnp.exp(-jnp.abs(a - b)))
    k = (1.0 - lb) / (1.0 + jnp.exp(zf))
    return logf, k


def _hgrn_prompt_kernel(zq_ref, zf_ref, zi_ref, zg_ref, lb_ref, go_ref, tri_ref, upper_ref,
                        o_ref, s_ref, st_scr, oi_scr):
    i = pl.program_id(1)
    tb = zq_ref.shape[0]

    @pl.when(i == 0)
    def _():
        st_scr[...] = jnp.zeros_like(st_scr)

    tri = tri_ref[...]
    upper = upper_ref[...]
    head_cols = [slice(h * HG_HEAD_DIM, (h + 1) * HG_HEAD_DIM) for h in range(HG_HEADS)]

    heads = []
    for cols in head_cols:
        v = zi_ref[:, cols]
        logf, k = _hgrn_gates(zf_ref[:, cols], lb_ref[:, cols])
        b = _dot_f32_rhs(tri, logf)
        rest = _dot_f32_rhs(upper, logf)
        q_in = (zq_ref[:, cols] * jnp.exp(b)).astype(BF16)
        k_in = (k * jnp.exp(-b)).astype(BF16)
        k_out = (k * jnp.exp(rest)).astype(BF16)
        dec = jnp.exp(b + rest)
        vb = v.astype(BF16)
        att = jnp.where(tri > 0, _dot_nt(q_in, k_in), 0.0).astype(BF16)
        o_intra = _dot(att, vb)
        heads.append((q_in, k_out, dec, vb, o_intra))

    sts = [st_scr[h] for h in range(HG_HEADS)]
    for n in range(tb // HG_CHUNK):
        rows = slice(n * HG_CHUNK, (n + 1) * HG_CHUNK)
        for h, cols in enumerate(head_cols):
            q_in, k_out, dec, vb, _ = heads[h]
            oi_scr[rows, cols] = _dot_nt(q_in[rows], sts[h].astype(BF16))
            update = lax.dot_general(vb[rows], k_out[rows], (((0,), (0,)), ((), ())),
                                     preferred_element_type=F32)
            sts[h] = sts[h] * dec[n * HG_CHUNK:n * HG_CHUNK + 1, :] + update

    go = go_ref[...]
    for h, cols in enumerate(head_cols):
        st_scr[h] = sts[h]
        o = heads[h][4] + oi_scr[:, cols]
        o_ref[:, cols] = _rms(o, go) * _silu(zg_ref[:, cols])

    @pl.when(i == pl.num_programs(1) - 1)
    def _():
        for h in range(HG_HEADS):
            s_ref[0, h] = sts[h].T


def _hgrn_prompt(z, lb, g_out, tri, upper, nb, seq):
    tb = MOBA_BLOCK
    nt = seq // tb
    zspec = lambda c: pl.BlockSpec((tb, HG_WIDTH), lambda b, i, c=c: (b * nt + i, c))
    sq = pl.BlockSpec((tb, tb), lambda b, i: (0, 0))
    return pl.pallas_call(
        _hgrn_prompt_kernel,
        grid=(nb, nt),
        in_specs=[zspec(COL_ZQ), zspec(COL_ZF), zspec(COL_ZI), zspec(COL_ZG),
                  pl.BlockSpec((1, HG_WIDTH), lambda b, i: (0, 0)),
                  pl.BlockSpec((1, HG_HEAD_DIM), lambda b, i: (0, 0)),
                  sq, sq],
        out_specs=[pl.BlockSpec((tb, HG_WIDTH), lambda b, i: (b * nt + i, 0)),
                   pl.BlockSpec((1, HG_HEADS, HG_HEAD_DIM, HG_HEAD_DIM), lambda b, i: (b, 0, 0, 0))],
        out_shape=[jax.ShapeDtypeStruct((nb * seq, HG_WIDTH), F32),
                   jax.ShapeDtypeStruct((nb, HG_HEADS, HG_HEAD_DIM, HG_HEAD_DIM), F32)],
        scratch_shapes=[pltpu.VMEM((HG_HEADS, HG_HEAD_DIM, HG_HEAD_DIM), F32),
                        pltpu.VMEM((tb, HG_WIDTH), F32)],
        compiler_params=_cparams(("parallel", "arbitrary")),
        name="hgrn_prompt",
    )(z, z, z, z, lb, g_out, tri, upper)


def _hgrn_sample_kernel(zq_ref, zf_ref, zi_ref, zg_ref, lb_ref, go_ref, s_ref, o_ref, sn_ref):
    db = zq_ref.shape[0]
    eye = (lax.broadcasted_iota(jnp.int32, (HG_HEAD_DIM, HG_HEAD_DIM), 0)
           == lax.broadcasted_iota(jnp.int32, (HG_HEAD_DIM, HG_HEAD_DIM), 1))
    col = lambda r: jnp.sum(jnp.where(eye, r, 0.0), axis=1, keepdims=True)
    lb = lb_ref[...]
    go = go_ref[...]

    def body(b, carry):
        row = pl.ds(b, 1)
        logf, k = _hgrn_gates(zf_ref[row, :], lb)
        s_new = s_ref[0, b, 0] * col(jnp.exp(logf)) + col(k) * zi_ref[row, :]
        sn_ref[b, 0] = s_new
        o = jnp.sum(s_new * col(zq_ref[row, :]), axis=0, keepdims=True)
        o_ref[row, :] = _rms(o, go) * _silu(zg_ref[row, :])
        return carry

    lax.fori_loop(0, db, body, 0)


def _hgrn_sample(z, lb, g_out, state, layer, row_blk, db):
    zspec = lambda c: pl.BlockSpec((db, HG_HEAD_DIM), lambda h, c=c: (row_blk, 4 * c + h))
    return pl.pallas_call(
        _hgrn_sample_kernel,
        grid=(HG_HEADS,),
        in_specs=[zspec(COL_ZQ), zspec(COL_ZF), zspec(COL_ZI), zspec(COL_ZG),
                  pl.BlockSpec((1, HG_HEAD_DIM), lambda h: (0, h)),
                  pl.BlockSpec((1, HG_HEAD_DIM), lambda h: (0, 0)),
                  pl.BlockSpec((1, db, 1, HG_HEAD_DIM, HG_HEAD_DIM), lambda h: (layer, 0, h, 0, 0))],
        out_specs=[pl.BlockSpec((db, HG_HEAD_DIM), lambda h: (0, h)),
                   pl.BlockSpec((db, 1, HG_HEAD_DIM, HG_HEAD_DIM), lambda h: (0, h, 0, 0))],
        out_shape=[jax.ShapeDtypeStruct((db, HG_WIDTH), F32),
                   jax.ShapeDtypeStruct((db, HG_HEADS, HG_HEAD_DIM, HG_HEAD_DIM), F32)],
        compiler_params=_cparams(("parallel",)),
        name="hgrn_sample",
    )(z, z, z, z, lb, g_out, state)


def _cache_mean_kernel(pt_ref, *refs):
    pages, o_ref = refs[:-1], refs[-1]
    g = pl.program_id(1)
    per_step = len(pages) // PAGES_PER_BLOCK
    ones = jnp.ones((8, PAGE_SIZE), BF16)
    for c in range(per_step):
        tot = pages[PAGES_PER_BLOCK * c][0, 0]
        for p in range(1, PAGES_PER_BLOCK):
            tot = tot + pages[PAGES_PER_BLOCK * c + p][0, 0]
        row = sum(_dot_nt(ones, t) for t in _split3(tot))[0:1]
        o_ref[0, pl.ds(g * per_step + c, 1), :] = row * (1.0 / MOBA_BLOCK)


def _cache_mean(page_table, cache_t, layer, db, n_blk):
    n_pages = n_blk * PAGES_PER_BLOCK
    pg = PAGE_GROUP if n_pages % PAGE_GROUP == 0 else PAGES_PER_BLOCK
    spec = lambda k: pl.BlockSpec((1, 1, ATT_WIDTH, PAGE_SIZE),
                                  lambda b, g, pt, k=k: (layer, pt[b, g * pg + k], 0, 0))
    return pl.pallas_call(
        _cache_mean_kernel,
        grid_spec=pltpu.PrefetchScalarGridSpec(
            num_scalar_prefetch=1,
            grid=(db, n_pages // pg),
            in_specs=[spec(k) for k in range(pg)],
            out_specs=pl.BlockSpec((1, n_blk, ATT_WIDTH), lambda b, g, pt: (b, 0, 0))),
        out_shape=jax.ShapeDtypeStruct((db, n_blk, ATT_WIDTH), F32),
        compiler_params=_cparams(("parallel", "arbitrary")),
        name="cache_block_mean",
    )(page_table, *([cache_t] * pg))


def _sample_select_kernel(q_ref, km_ref, hm_ref, o_ref):
    b = pl.program_id(0)
    km = km_ref[0].astype(BF16).astype(F32)
    q = q_ref[pl.ds(b, 1), :].astype(BF16).astype(F32)
    gate = _dot_f32_lhs(km * q, hm_ref[...])
    _, picks = _top_mask_axis0(gate, MOBA_TOPK)
    rows = picks + [jnp.zeros_like(picks[0])] * (o_ref.shape[1] - len(picks))
    o_ref[0] = jnp.concatenate(rows, axis=0)


def _sample_select(q_s, km_s, head_mat):
    db, n_blk, _ = km_s.shape
    return pl.pallas_call(
        _sample_select_kernel,
        grid=(db,),
        in_specs=[pl.BlockSpec((db, ATT_WIDTH), lambda b: (0, 0)),
                  pl.BlockSpec((1, n_blk, ATT_WIDTH), lambda b: (b, 0, 0)),
                  pl.BlockSpec((ATT_WIDTH, LANES), lambda b: (0, 0))],
        out_specs=pl.BlockSpec((1, 8, LANES), lambda b: (b, 0, 0)),
        out_shape=jax.ShapeDtypeStruct((db, 8, LANES), jnp.int32),
        compiler_params=_cparams(("parallel",)),
        name="sample_select",
    )(q_s, km_s, head_mat)


def _moba_sample_kernel(past_len, phys_ref, blk_ref, slopes_ref, q_ref, k_ref, v_ref, *refs):
    pages_per_head = MOBA_TOPK * PAGES_PER_BLOCK
    n_pages = SAMPLE_HEADS_PER_STEP * pages_per_head
    k_pages, v_pages, o_ref = refs[:n_pages], refs[n_pages:2 * n_pages], refs[-1]
    b = pl.program_id(0)
    m_keys = MOBA_TOPK * MOBA_BLOCK
    lane = lax.broadcasted_iota(jnp.int32, (1, m_keys), 1)
    for u in range(SAMPLE_HEADS_PER_STEP):
        h = pl.program_id(1) * SAMPLE_HEADS_PER_STEP + u
        head = pl.ds(h, 1)
        mine = slice(u * pages_per_head, (u + 1) * pages_per_head)
        q = q_ref[0, head, :] * (HEAD_DIM ** -0.5)
        kt = jnp.concatenate([r[0, 0, 0] for r in k_pages[mine]], axis=1).astype(BF16)
        vt = jnp.concatenate([r[0, 0, 0] for r in v_pages[mine]], axis=1).astype(BF16)
        s = _dot(jnp.broadcast_to(q, (8, HEAD_DIM)).astype(BF16), kt)[0:1]
        kblk = blk_ref[b, h * MOBA_TOPK + MOBA_TOPK - 1]
        for t in range(MOBA_TOPK - 2, -1, -1):
            kblk = jnp.where(lane < (t + 1) * MOBA_BLOCK, blk_ref[b, h * MOBA_TOPK + t], kblk)
        kpos = kblk * MOBA_BLOCK + lane % MOBA_BLOCK
        s = s - slopes_ref[h] * (past_len - kpos).astype(F32)
        s_self = jnp.sum(q * k_ref[0, head, :], axis=1, keepdims=True)
        m = jnp.maximum(jnp.max(s, axis=1, keepdims=True), s_self)
        p = jnp.exp(s - m)
        p_self = jnp.exp(s_self - m)
        l = jnp.sum(p, axis=1, keepdims=True) + p_self
        o = _dot_nt(jnp.broadcast_to(p, (8, m_keys)).astype(BF16), vt)[0:1] + p_self * v_ref[0, head, :]
        o_ref[0, u] = o / l


def _moba_sample(phys, sel_blk, slopes, q_s, k_s, v_s, cache_kt, cache_vt, layer, past_len):
    db = q_s.shape[0]
    pages_per_head = MOBA_TOPK * PAGES_PER_BLOCK
    hps = SAMPLE_HEADS_PER_STEP
    n_pages = hps * pages_per_head
    row = pl.BlockSpec((1, N_HEADS, HEAD_DIM), lambda b, g, ph, sb, sl: (b, 0, 0))
    page = lambda t: pl.BlockSpec(
        (1, 1, 1, HEAD_DIM, PAGE_SIZE),
        lambda b, g, ph, sb, sl, t=t: (layer, ph[b, g * n_pages + t], g * hps + t // pages_per_head, 0, 0))
    pages = [page(t) for t in range(n_pages)]
    r3 = lambda a: a.reshape(db, N_HEADS, HEAD_DIM)
    out = pl.pallas_call(
        functools.partial(_moba_sample_kernel, past_len),
        grid_spec=pltpu.PrefetchScalarGridSpec(
            num_scalar_prefetch=3,
            grid=(db, N_HEADS // hps),
            in_specs=[row, row, row] + pages + pages,
            out_specs=pl.BlockSpec((1, hps, 1, HEAD_DIM), lambda b, g, ph, sb, sl: (b, g, 0, 0))),
        out_shape=jax.ShapeDtypeStruct((db, N_HEADS, 1, HEAD_DIM), F32),
        compiler_params=_cparams(("parallel", "arbitrary")),
        name="moba_sample",
    )(phys, sel_blk, slopes, r3(q_s), r3(k_s), r3(v_s), *([cache_kt] * n_pages), *([cache_vt] * n_pages))
    return out.reshape(db, ATT_WIDTH)


def _merge_kernel(ob_transposed, x_ref, oa_ref, ob_ref, ga0, ga1, gb0, gb1, wa_ref, wb_ref, wo_ref, o_ref):
    oa = oa_ref[...].astype(BF16)
    ob = ob_ref[0].T if ob_transposed else ob_ref[...]
    a = _dot(oa, wa_ref[...])
    bm = _dot(ob.astype(BF16), wb_ref[...])
    half = D_MODEL // 2
    m0 = _sigmoid(ga0[...]) * a[:, :half] + _sigmoid(gb0[...]) * bm[:, :half]
    m1 = _sigmoid(ga1[...]) * a[:, half:] + _sigmoid(gb1[...]) * bm[:, half:]
    y = _dot(m0.astype(BF16), wo_ref[:half, :]) + _dot(m1.astype(BF16), wo_ref[half:, :])
    o_ref[...] = x_ref[...] + y


def _merge(x, z, o_a, o_b, wa, wb, wo, *, tile, row_blk0, n_tiles, ob_transposed, tiles_per_seq=None):
    rspec = lambda w, c: pl.BlockSpec((tile, w), lambda i, c=c: (row_blk0 + i, c))
    if ob_transposed:
        ob_spec = pl.BlockSpec((1, ATT_WIDTH, tile), lambda i: (i // tiles_per_seq, 0, i % tiles_per_seq))
    else:
        ob_spec = pl.BlockSpec((tile, ATT_WIDTH), lambda i: (i, 0))
    wspec = lambda r: pl.BlockSpec((r, D_MODEL), lambda i: (0, 0))
    return pl.pallas_call(
        functools.partial(_merge_kernel, ob_transposed),
        grid=(n_tiles,),
        in_specs=[rspec(D_MODEL, 0),
                  pl.BlockSpec((tile, HG_WIDTH), lambda i: (i, 0)), ob_spec,
                  rspec(512, COL_GA), rspec(512, COL_GA + 1), rspec(512, COL_GB), rspec(512, COL_GB + 1),
                  wspec(HG_WIDTH), wspec(ATT_WIDTH), wspec(D_MODEL)],
        out_specs=rspec(D_MODEL, 0),
        out_shape=jax.ShapeDtypeStruct(x.shape, F32),
        input_output_aliases={0: 0},
        compiler_params=_cparams(("parallel",)),
        name="merge_t" if ob_transposed else "merge",
    )(x, o_a, o_b, z, z, z, z, wa, wb, wo)


def _ffn_kernel(x_ref, g_ref, wg_ref, wu_ref, wd_ref, o_ref, h_scr, acc_scr):
    j = pl.program_id(1)

    @pl.when(j == 0)
    def _():
        h_scr[...] = _rms(x_ref[...], g_ref[...]).astype(BF16)
        acc_scr[...] = jnp.zeros_like(acc_scr)

    h = h_scr[...]
    a = _silu(_dot(h, wg_ref[...])) * _dot(h, wu_ref[...])
    acc_scr[...] += _dot(a.astype(BF16), wd_ref[...])

    @pl.when(j == pl.num_programs(1) - 1)
    def _():
        o_ref[...] = x_ref[...] + acc_scr[...]


def _ffn_dense(x, g, wg, wu, wd):
    n = x.shape[0]
    tm = _row_tile(n, 512)
    return pl.pallas_call(
        _ffn_kernel,
        grid=(pl.cdiv(n, tm), D_FF // FF_TILE),
        in_specs=[pl.BlockSpec((tm, D_MODEL), lambda i, j: (i, 0)),
                  pl.BlockSpec((1, D_MODEL), lambda i, j: (0, 0)),
                  pl.BlockSpec((D_MODEL, FF_TILE), lambda i, j: (0, j)),
                  pl.BlockSpec((D_MODEL, FF_TILE), lambda i, j: (0, j)),
                  pl.BlockSpec((FF_TILE, D_MODEL), lambda i, j: (j, 0))],
        out_specs=pl.BlockSpec((tm, D_MODEL), lambda i, j: (i, 0)),
        out_shape=jax.ShapeDtypeStruct((n, D_MODEL), F32),
        scratch_shapes=[pltpu.VMEM((tm, D_MODEL), BF16), pltpu.VMEM((tm, D_MODEL), F32)],
        compiler_params=_cparams(("parallel", "arbitrary")),
        name="ffn_dense",
    )(x, g.reshape(1, D_MODEL), wg, wu, wd)


def _router_kernel(x_ref, g_ref, wr_ref, h_ref, logit_ref):
    h = _rms(x_ref[...], g_ref[...])
    h_ref[...] = h
    logit_ref[...] = _dot(h.astype(BF16), wr_ref[...].astype(BF16))


def _router(x, g, wr_pad):
    n = x.shape[0]
    tm = _row_tile(n, 1024)
    return pl.pallas_call(
        _router_kernel,
        grid=(pl.cdiv(n, tm),),
        in_specs=[pl.BlockSpec((tm, D_MODEL), lambda i: (i, 0)),
                  pl.BlockSpec((1, D_MODEL), lambda i: (0, 0)),
                  pl.BlockSpec((D_MODEL, LANES), lambda i: (0, 0))],
        out_specs=[pl.BlockSpec((tm, D_MODEL), lambda i: (i, 0)),
                   pl.BlockSpec((tm, LANES), lambda i: (i, 0))],
        out_shape=[jax.ShapeDtypeStruct((n, D_MODEL), F32),
                   jax.ShapeDtypeStruct((n, LANES), F32)],
        compiler_params=_cparams(("parallel",)),
        name="moe_router",
    )(x, g.reshape(1, D_MODEL), wr_pad)


def _expert_kernel(be_ref, used_ref, x_ref, wg_ref, wu_ref, wd_ref, o_ref, acc_scr):
    i = pl.program_id(0)
    j = pl.program_id(1)

    @pl.when(j == 0)
    def _():
        acc_scr[...] = jnp.zeros_like(acc_scr)

    @pl.when(used_ref[i] > 0)
    def _():
        x = x_ref[...].astype(BF16)
        a = _silu(_dot(x, wg_ref[0])) * _dot(x, wu_ref[0])
        acc_scr[...] += _dot(a.astype(BF16), wd_ref[0])

    @pl.when(j == pl.num_programs(1) - 1)
    def _():
        o_ref[...] = acc_scr[...]


def _experts(blk_e, blk_used, xb, weg, weu, wed):
    rows = xb.shape[0] // MOE_ROWS * MOE_ROWS
    return pl.pallas_call(
        _expert_kernel,
        grid_spec=pltpu.PrefetchScalarGridSpec(
            num_scalar_prefetch=2,
            grid=(rows // MOE_ROWS, D_FF // FF_TILE),
            in_specs=[pl.BlockSpec((MOE_ROWS, D_MODEL), lambda i, j, be, bu: (i, 0)),
                      pl.BlockSpec((1, D_MODEL, FF_TILE), lambda i, j, be, bu: (be[i], 0, j)),
                      pl.BlockSpec((1, D_MODEL, FF_TILE), lambda i, j, be, bu: (be[i], 0, j)),
                      pl.BlockSpec((1, FF_TILE, D_MODEL), lambda i, j, be, bu: (be[i], j, 0))],
            out_specs=pl.BlockSpec((MOE_ROWS, D_MODEL), lambda i, j, be, bu: (i, 0)),
            scratch_shapes=[pltpu.VMEM((MOE_ROWS, D_MODEL), F32)]),
        out_shape=jax.ShapeDtypeStruct((rows, D_MODEL), F32),
        compiler_params=_cparams(("parallel", "arbitrary")),
        name="moe_experts",
    )(blk_e, blk_used, xb, weg, weu, wed)


def _row_copy(src, src_row, dst, dst_row, sem):
    return pltpu.make_async_copy(src.at[pl.ds(src_row, 1)], dst.at[pl.ds(dst_row, 1)], sem)


def _dispatch_kernel(dest_ref, h_ref, xb_in_ref, xb_ref, sem):
    del xb_in_ref
    tile = h_ref.shape[0]

    def start(r, carry):
        for k in range(TOP_K):
            _row_copy(h_ref, r, xb_ref, dest_ref[0, 0, r * TOP_K + k], sem).start()
        return carry

    def wait(r, carry):
        for k in range(TOP_K):
            _row_copy(h_ref, r, xb_ref, dest_ref[0, 0, r * TOP_K + k], sem).wait()
        return carry

    lax.fori_loop(0, tile, start, 0, unroll=DMA_UNROLL)
    lax.fori_loop(0, tile, wait, 0, unroll=DMA_UNROLL)


def _dispatch(dest, h, n_rows):
    n = h.shape[0]
    tile = _row_tile(n, 512)
    n_tiles = pl.cdiv(n, tile)
    dest_p = jnp.pad(dest, ((0, n_tiles * tile - n), (0, 0)), constant_values=n_rows)
    hbm = pl.BlockSpec(memory_space=pl.ANY)
    return pl.pallas_call(
        _dispatch_kernel,
        grid=(n_tiles,),
        in_specs=[pl.BlockSpec((1, 1, tile * TOP_K), lambda i: (i, 0, 0), memory_space=pltpu.SMEM),
                  pl.BlockSpec((tile, D_MODEL), lambda i: (i, 0)), hbm],
        out_specs=hbm,
        out_shape=jax.ShapeDtypeStruct((n_rows + 8, D_MODEL), F32),
        scratch_shapes=[pltpu.SemaphoreType.DMA(())],
        input_output_aliases={2: 0},
        compiler_params=_cparams(("arbitrary",)),
        name="moe_dispatch",
    )(dest_p.reshape(n_tiles, 1, tile * TOP_K), h, jnp.zeros((n_rows + 8, D_MODEL), F32))


def _combine_kernel(slot_ref, x_ref, gate_ref, yb_ref, o_ref, ybuf, sem):
    tile = x_ref.shape[0]

    def start(r, carry):
        for k in range(TOP_K):
            _row_copy(yb_ref, slot_ref[0, 0, r * TOP_K + k], ybuf.at[k], r, sem).start()
        return carry

    def wait(r, carry):
        for k in range(TOP_K):
            _row_copy(yb_ref, slot_ref[0, 0, r * TOP_K + k], ybuf.at[k], r, sem).wait()
        return carry

    lax.fori_loop(0, tile, start, 0, unroll=DMA_UNROLL)
    lax.fori_loop(0, tile, wait, 0, unroll=DMA_UNROLL)
    y = x_ref[...]
    for k in range(TOP_K):
        y = y + gate_ref[:, k:k + 1] * ybuf[k]
    o_ref[...] = y


def _combine(x, gates, dest, yb):
    n = x.shape[0]
    tile = _row_tile(n, 512)
    n_tiles = pl.cdiv(n, tile)
    slot_p = jnp.pad(dest, ((0, n_tiles * tile - n), (0, 0))).reshape(n_tiles, 1, tile * TOP_K)
    return pl.pallas_call(
        _combine_kernel,
        grid=(n_tiles,),
        in_specs=[pl.BlockSpec((1, 1, tile * TOP_K), lambda i: (i, 0, 0), memory_space=pltpu.SMEM),
                  pl.BlockSpec((tile, D_MODEL), lambda i: (i, 0)),
                  pl.BlockSpec((tile, TOP_K), lambda i: (i, 0)),
                  pl.BlockSpec(memory_space=pl.ANY)],
        out_specs=pl.BlockSpec((tile, D_MODEL), lambda i: (i, 0)),
        out_shape=jax.ShapeDtypeStruct((n, D_MODEL), F32),
        scratch_shapes=[pltpu.VMEM((TOP_K, tile, D_MODEL), F32), pltpu.SemaphoreType.DMA(())],
        compiler_params=_cparams(("arbitrary",)),
        name="moe_combine",
    )(slot_p, x, gates, yb)


def _moe(x, g, wr, weg, weu, wed):
    n = x.shape[0]
    wr_pad = jnp.zeros((D_MODEL, LANES), F32).at[:, :N_EXPERTS].set(wr)
    h, logits = _router(x, g, wr_pad)
    top_l, top_e = lax.top_k(logits[:, :N_EXPERTS], TOP_K)
    gates = jax.nn.softmax(top_l, axis=-1)
    n_slots = n * TOP_K
    n_blocks = -(-(n_slots + N_EXPERTS * (MOE_ROWS - 1)) // MOE_ROWS)
    onehot = (top_e.reshape(n_slots, 1) == jnp.arange(N_EXPERTS)).astype(jnp.int32)
    upto = jnp.cumsum(onehot, axis=0)
    counts = upto[-1]
    padded = (counts + MOE_ROWS - 1) // MOE_ROWS * MOE_ROWS
    end_pad = jnp.cumsum(padded)
    start_pad = end_pad - padded
    dest = jnp.sum(onehot * (start_pad + upto - 1), axis=1).astype(jnp.int32).reshape(n, TOP_K)
    blk_start = jnp.arange(n_blocks) * MOE_ROWS
    blk_e = jnp.minimum(jnp.searchsorted(end_pad, blk_start, side='right'), N_EXPERTS - 1).astype(jnp.int32)
    blk_used = (blk_start < end_pad[-1]).astype(jnp.int32)
    xb = _dispatch(dest, h, n_blocks * MOE_ROWS)
    yb = _experts(blk_e, blk_used, xb, weg, weu, wed)
    return _combine(x, gates, dest, yb)


def _chunk_masks(tb):
    t = np.arange(tb)
    same = (t[:, None] // HG_CHUNK) == (t[None, :] // HG_CHUNK)
    tri = same & (t[None, :] <= t[:, None])
    upper = same & (t[None, :] > t[:, None])
    return jnp.asarray(tri, BF16), jnp.asarray(upper, BF16)


def kernel(x_prompt, x_sample, cache_k, cache_v, state_hgrn, page_table, w_in, hg_lower, hg_norm, q_norm, k_norm, w_branch_a, w_branch_b, w_out, attn_norm, ffn_norm, w_dense_gate, w_dense_up, w_dense_down, w_router, w_exp_gate, w_exp_up, w_exp_down):
    nb, seq, _ = x_prompt.shape
    db, dec_seq, _ = x_sample.shape
    depth, n_pool = cache_k.shape[:2]
    n_pages = page_table.shape[1]
    past_len = n_pages * PAGE_SIZE
    assert dec_seq == 1 and seq % MOBA_BLOCK == 0 and past_len % MOBA_BLOCK == 0
    n_rows_p = nb * seq
    assert n_rows_p % db == 0 and db % 8 == 0
    nblk = seq // MOBA_BLOCK
    n_blk_s = past_len // MOBA_BLOCK
    assert MOBA_TOPK <= nblk <= MAX_BLOCKS and n_blk_s >= MOBA_TOPK
    row_blk_s = n_rows_p // db

    slopes = 2.0 ** (-8.0 * jnp.arange(1, N_HEADS + 1, dtype=F32) / N_HEADS)
    lb_cum = jnp.cumsum(jax.nn.softmax(hg_lower.astype(F32), axis=0), axis=0)
    lower = lb_cum - lb_cum[:1]

    lane = np.arange(ATT_WIDTH)
    gmat = jnp.asarray((lane[:, None] // HEAD_DIM) == (lane[None, :] // HEAD_DIM), BF16)
    head_mat = jnp.asarray((lane[:, None] // HEAD_DIM) == np.arange(LANES)[None, :], BF16)
    tri, upper = _chunk_masks(MOBA_BLOCK)
    cache_kt = cache_k.transpose(0, 1, 3, 4, 2)
    cache_vt = cache_v.transpose(0, 1, 3, 4, 2)
    cache_kt4 = cache_kt.reshape(depth, n_pool, ATT_WIDTH, PAGE_SIZE)

    x = jnp.concatenate([x_prompt.reshape(n_rows_p, D_MODEL), x_sample.reshape(db, D_MODEL)], axis=0)
    outs = {name: [] for name in ("kp", "vp", "sp", "ks", "vs", "ss")}
    for l in range(depth):
        z = _proj(x, attn_norm[l], w_in[l].astype(BF16))
        lb = lower[l].reshape(1, HG_WIDTH)
        g_out = hg_norm[l].reshape(1, HG_HEAD_DIM)
        qn = jnp.tile(q_norm[l], N_HEADS).reshape(1, ATT_WIDTH)
        kn = jnp.tile(k_norm[l], N_HEADS).reshape(1, ATT_WIDTH)
        wa, wb, wo = (w[l].astype(BF16) for w in (w_branch_a, w_branch_b, w_out))

        oa_p, s_p = _hgrn_prompt(z, lb, g_out, tri, upper, nb, seq)
        k_p, v_p, qt, kh, vt, km = _qk_prompt(z, qn, kn, gmat, nb, nblk)
        kmh = km.reshape(nb, nblk, N_HEADS, HEAD_DIM).transpose(0, 2, 1, 3)
        kmh = jnp.pad(kmh, ((0, 0), (0, 0), (0, MAX_BLOCKS - nblk), (0, 0)))
        ob_p = _moba_prompt(slopes, qt, kh, vt, kmh).reshape(nb, ATT_WIDTH, seq)

        oa_s, s_s = _hgrn_sample(z, lb, g_out, state_hgrn, l, row_blk_s, db)
        q_s, k_s, v_s = _qk_sample(z, qn, kn, gmat, row_blk_s, db)
        km_s = _cache_mean(page_table, cache_kt4, l, db, n_blk_s)
        sel = _sample_select(q_s, km_s, head_mat)[:, :MOBA_TOPK, :N_HEADS]
        sel_blk = sel.transpose(0, 2, 1)
        lpage = sel_blk[..., None] * PAGES_PER_BLOCK + jnp.arange(PAGES_PER_BLOCK)
        phys = jnp.take_along_axis(page_table, lpage.reshape(db, -1), axis=1)
        ob_s = _moba_sample(phys, sel_blk.reshape(db, -1), slopes, q_s, k_s, v_s, cache_kt, cache_vt,
                            l, past_len)

        mt = 2 * MOBA_BLOCK if seq % (2 * MOBA_BLOCK) == 0 else MOBA_BLOCK
        x = _merge(x, z, oa_p, ob_p, wa, wb, wo, tile=mt, row_blk0=0, n_tiles=n_rows_p // mt,
                   ob_transposed=True, tiles_per_seq=seq // mt)
        x = _merge(x, z, oa_s, ob_s, wa, wb, wo, tile=db, row_blk0=row_blk_s, n_tiles=1,
                   ob_transposed=False)

        i = l // 2
        if l % 2 == 0:
            x = _ffn_dense(x, ffn_norm[l], w_dense_gate[i].astype(BF16), w_dense_up[i].astype(BF16),
                           w_dense_down[i].astype(BF16))
        else:
            x = _moe(x, ffn_norm[l], w_router[i], w_exp_gate[i].astype(BF16), w_exp_up[i].astype(BF16),
                     w_exp_down[i].astype(BF16))

        outs["kp"].append(k_p)
        outs["vp"].append(v_p)
        outs["sp"].append(s_p)
        outs["ks"].append(k_s.reshape(db, 1, N_HEADS, HEAD_DIM))
        outs["vs"].append(v_s.reshape(db, 1, N_HEADS, HEAD_DIM))
        outs["ss"].append(s_s)

    seq_major = lambda rows: jnp.stack(rows).transpose(0, 1, 4, 2, 3)
    return (x[:n_rows_p].reshape(nb, seq, D_MODEL), x[n_rows_p:].reshape(db, 1, D_MODEL),
            seq_major(outs["kp"]), seq_major(outs["vp"]), jnp.stack(outs["sp"]),
            jnp.stack(outs["ks"]), jnp.stack(outs["vs"]), jnp.stack(outs["ss"]))
```

```python
import functools

import numpy as np
import jax
import jax.numpy as jnp
from jax import lax
from jax.experimental import pallas as pl
from jax.experimental.pallas import tpu as pltpu

F32 = jnp.float32
BF16 = jnp.bfloat16

D_MODEL = 1024
HG_WIDTH = 512
HG_HEAD_DIM = 128
HG_HEADS = HG_WIDTH // HG_HEAD_DIM
HG_CHUNK = 16
HEAD_DIM = 64
N_HEADS = 8
ATT_WIDTH = N_HEADS * HEAD_DIM
MOBA_BLOCK = 256
MOBA_TOPK = 3
PAGE_SIZE = 128
PAGES_PER_BLOCK = MOBA_BLOCK // PAGE_SIZE
D_FF = 2816
N_EXPERTS = 8
TOP_K = 2
RMS_EPS = 1e-6
IN_WIDTH = 4 * HG_WIDTH + 3 * ATT_WIDTH + 2 * D_MODEL
COL_ZQ, COL_ZF, COL_ZI, COL_ZG, COL_AQ, COL_AK, COL_AV = range(7)
COL_GA, COL_GB = 7, 9

KEY_WIDTH = 2 * HEAD_DIM
MAX_BLOCKS = 32
FEAT_OFF = MAX_BLOCKS + 3
FEAT_BLK = MAX_BLOCKS + 6
FEAT_END = MAX_BLOCKS + 12
LOG2E = 1.4426950408889634
VAL_ROWS = HEAD_DIM + 16
BLOCKS_PER_PASS = 2

MASKED = -1e30
VMEM_LIMIT = 56 * 1024 * 1024
FF_TILE = D_FF // 2
MOE_ROWS = 512
DMA_UNROLL = 8
PAGE_GROUP = 16
SAMPLE_HEADS_PER_STEP = 2
LANES = 128


def _cparams(sem):
    return pltpu.CompilerParams(dimension_semantics=sem, vmem_limit_bytes=VMEM_LIMIT)


def _row_tile(n, hi):
    for t in range(hi, hi // 4, -16):
        if n % t == 0:
            return t
    return hi


def _split3(a):
    a1 = a.astype(BF16)
    r = a - a1.astype(F32)
    a2 = r.astype(BF16)
    a3 = (r - a2.astype(F32)).astype(BF16)
    return a1, a2, a3


def _dot(a, b):
    return jnp.dot(a, b, preferred_element_type=F32)


def _dot_nt(a, b):
    return lax.dot_general(a, b, (((1,), (1,)), ((), ())), preferred_element_type=F32)


def _dot_f32_lhs(a, b_exact):
    a1, a2, a3 = _split3(a)
    return _dot(a1, b_exact) + _dot(a2, b_exact) + _dot(a3, b_exact)


def _dot_f32_rhs(a_exact, b):
    b1, b2, b3 = _split3(b)
    return _dot(a_exact, b1) + _dot(a_exact, b2) + _dot(a_exact, b3)


def _rms(x, g):
    return x * lax.rsqrt(jnp.mean(x * x, axis=-1, keepdims=True) + RMS_EPS) * g


def _silu(x):
    return x / (1.0 + jnp.exp(-x))


def _sigmoid(x):
    return 1.0 / (1.0 + jnp.exp(-x))


def _top_mask_axis0(g, k):
    idx = lax.broadcasted_iota(jnp.int32, g.shape, 0)
    sel = jnp.zeros(g.shape, jnp.bool_)
    picks = []
    for _ in range(k):
        mx = jnp.max(g, axis=0, keepdims=True)
        first = jnp.min(jnp.where(g == mx, idx, g.shape[0]), axis=0, keepdims=True)
        hit = (idx == first) & (mx > -jnp.inf)
        sel = sel | hit
        g = jnp.where(idx == first, -jnp.inf, g)
        picks.append(first)
    return sel, picks


def _proj_kernel(x_ref, g_ref, w_ref, o_ref, h_scr):
    @pl.when(pl.program_id(1) == 0)
    def _():
        h_scr[...] = _rms(x_ref[...], g_ref[...]).astype(BF16)

    o_ref[...] = _dot(h_scr[...], w_ref[...])


def _proj(x, g, w_bf16):
    n = x.shape[0]
    tm = _row_tile(n, 1024)
    tn = IN_WIDTH // 4
    return pl.pallas_call(
        _proj_kernel,
        grid=(pl.cdiv(n, tm), IN_WIDTH // tn),
        in_specs=[pl.BlockSpec((tm, D_MODEL), lambda i, j: (i, 0)),
                  pl.BlockSpec((1, D_MODEL), lambda i, j: (0, 0)),
                  pl.BlockSpec((D_MODEL, tn), lambda i, j: (0, j))],
        out_specs=pl.BlockSpec((tm, tn), lambda i, j: (i, j)),
        out_shape=jax.ShapeDtypeStruct((n, IN_WIDTH), F32),
        scratch_shapes=[pltpu.VMEM((tm, D_MODEL), BF16)],
        compiler_params=_cparams(("parallel", "arbitrary")),
        name="in_proj",
    )(x, g.reshape(1, D_MODEL), w_bf16)


def _group_rms(x, gain, gmat):
    ms = _dot_f32_lhs(x * x, gmat) * (1.0 / HEAD_DIM)
    return x * lax.rsqrt(ms + RMS_EPS) * gain


def _qk_prompt_kernel(aq_ref, ak_ref, av_ref, qn_ref, kn_ref, gm_ref,
                      kt_out, vt_out, qt_out, kh_out, vtb_out, km_out):
    j = pl.program_id(1)
    gmat = gm_ref[...]
    q = _group_rms(aq_ref[...], qn_ref[...], gmat)
    k = _group_rms(ak_ref[...], kn_ref[...], gmat)
    v = av_ref[...]
    vt = v.T.reshape(N_HEADS, HEAD_DIM, MOBA_BLOCK)
    qt_out[0] = q.T.reshape(N_HEADS, HEAD_DIM, MOBA_BLOCK)
    kt_out[0] = k.T.reshape(N_HEADS, HEAD_DIM, MOBA_BLOCK)
    vt_out[0] = vt
    ones = jnp.ones((N_HEADS, VAL_ROWS - HEAD_DIM, MOBA_BLOCK), BF16)
    vtb_out[0, :, 0] = jnp.concatenate([vt.astype(BF16), ones], axis=1)
    lane = lax.broadcasted_iota(jnp.int32, (MOBA_BLOCK, HEAD_DIM), 1)
    off = lax.broadcasted_iota(jnp.int32, (MOBA_BLOCK, HEAD_DIM), 0).astype(F32)
    feat = jnp.where(lane < FEAT_OFF, off,
                     jnp.where(lane < FEAT_BLK, j.astype(F32), jnp.where(lane < FEAT_END, 1.0, 0.0)))
    ext = jnp.where(lane < MAX_BLOCKS, (lane == j).astype(F32), feat).astype(BF16)
    kb = k.astype(BF16)
    for h in range(N_HEADS):
        kh_out[0, h, 0] = jnp.concatenate([kb[:, h * HEAD_DIM:(h + 1) * HEAD_DIM], ext], axis=1)
    km_out[0, 0] = jnp.sum(k, axis=0, keepdims=True) * (1.0 / MOBA_BLOCK)


def _qk_prompt(z, qn, kn, gmat, nb, nblk):
    seq = nblk * MOBA_BLOCK
    zspec = lambda c: pl.BlockSpec((MOBA_BLOCK, ATT_WIDTH), lambda b, i, c=c: (b * nblk + i, c))
    vec = pl.BlockSpec((1, ATT_WIDTH), lambda b, i: (0, 0))
    head5 = lambda r, c: pl.BlockSpec((1, N_HEADS, 1, r, c), lambda b, i: (b, 0, i, 0, 0))
    tspec = pl.BlockSpec((1, N_HEADS, HEAD_DIM, MOBA_BLOCK), lambda b, i: (b, 0, 0, i))
    tshape = jax.ShapeDtypeStruct((nb, N_HEADS, HEAD_DIM, seq), F32)
    return pl.pallas_call(
        _qk_prompt_kernel,
        grid=(nb, nblk),
        in_specs=[zspec(COL_AQ), zspec(COL_AK), zspec(COL_AV), vec, vec,
                  pl.BlockSpec((ATT_WIDTH, ATT_WIDTH), lambda b, i: (0, 0))],
        out_specs=[tspec, tspec, tspec,
                   head5(MOBA_BLOCK, KEY_WIDTH), head5(VAL_ROWS, MOBA_BLOCK),
                   pl.BlockSpec((1, 1, 1, ATT_WIDTH), lambda b, i: (b, i, 0, 0))],
        out_shape=[tshape, tshape, tshape,
                   jax.ShapeDtypeStruct((nb, N_HEADS, nblk, MOBA_BLOCK, KEY_WIDTH), BF16),
                   jax.ShapeDtypeStruct((nb, N_HEADS, nblk, VAL_ROWS, MOBA_BLOCK), BF16),
                   jax.ShapeDtypeStruct((nb, nblk, 1, ATT_WIDTH), F32)],
        compiler_params=_cparams(("parallel", "parallel")),
        name="qk_prompt",
    )(z, z, z, qn, kn, gmat)


def _qk_sample_kernel(aq_ref, ak_ref, av_ref, qn_ref, kn_ref, gm_ref, q_out, k_out, v_out):
    gmat = gm_ref[...]
    q_out[...] = _group_rms(aq_ref[...], qn_ref[...], gmat)
    k_out[...] = _group_rms(ak_ref[...], kn_ref[...], gmat)
    v_out[...] = av_ref[...]


def _qk_sample(z, qn, kn, gmat, row_blk, db):
    zspec = lambda c: pl.BlockSpec((db, ATT_WIDTH), lambda i, c=c: (row_blk, c))
    vec = pl.BlockSpec((1, ATT_WIDTH), lambda i: (0, 0))
    out = pl.BlockSpec((db, ATT_WIDTH), lambda i: (0, 0))
    shp = jax.ShapeDtypeStruct((db, ATT_WIDTH), F32)
    return pl.pallas_call(
        _qk_sample_kernel,
        grid=(1,),
        in_specs=[zspec(COL_AQ), zspec(COL_AK), zspec(COL_AV), vec, vec,
                  pl.BlockSpec((ATT_WIDTH, ATT_WIDTH), lambda i: (0, 0))],
        out_specs=[out, out, out],
        out_shape=[shp, shp, shp],
        compiler_params=_cparams(("arbitrary",)),
        name="qk_sample",
    )(z, z, z, qn, kn, gmat)


def _moba_prompt_kernel(slopes_ref, qt_ref, kh_ref, vt_ref, km_ref, o_ref,
                        qa_scr, s_scr, p_scr, m_scr, acc_scr):
    i = pl.program_id(1)
    heads = range(N_HEADS)
    n_feat_rows = KEY_WIDTH - HEAD_DIM - MAX_BLOCKS
    qry_pos = lax.broadcasted_iota(jnp.int32, (1, MOBA_BLOCK), 1).astype(F32)
    feat_row = lax.broadcasted_iota(jnp.int32, (n_feat_rows, MOBA_BLOCK), 0)
    blk = lax.broadcasted_iota(jnp.int32, (MAX_BLOCKS, MOBA_BLOCK), 0)
    causal = (lax.broadcasted_iota(jnp.int32, (MOBA_BLOCK, MOBA_BLOCK), 0)
              <= lax.broadcasted_iota(jnp.int32, (MOBA_BLOCK, MOBA_BLOCK), 1))
    ones = jnp.ones((1, MOBA_BLOCK), F32)

    gates = [_dot(km_ref[0, h].astype(BF16), qt_ref[0, h].astype(BF16)) for h in heads]
    for h in heads:
        qt = qt_ref[0, h]
        sel, _ = _top_mask_axis0(jnp.where(blk < i, gates[h], -jnp.inf), MOBA_TOPK)
        sel_bias = jnp.where(sel | (blk >= i), 0.0, MASKED)
        c = slopes_ref[h] * LOG2E
        coefs = (c * ones, (c * MOBA_BLOCK) * ones, -c * qry_pos,
                 (-(c * MOBA_BLOCK) * i.astype(F32)) * ones)
        feat = jnp.zeros((n_feat_rows, MOBA_BLOCK), F32)
        for r, term in enumerate(t for v in coefs for t in _split3(v)):
            feat = jnp.where(feat_row == r, term.astype(F32), feat)
        q_aug = jnp.concatenate([(qt * (HEAD_DIM ** -0.5 * LOG2E)).astype(BF16), sel_bias.astype(BF16),
                                 feat.astype(BF16)], axis=0)
        qa_scr[h] = q_aug

    for h in heads:
        s_scr[h] = jnp.where(causal, _dot(kh_ref[0, h, i], qa_scr[h]), MASKED)
    for h in heads:
        s = s_scr[h]
        m = jnp.max(s, axis=0, keepdims=True)
        p = jnp.exp2(s - m)
        p_scr[h] = p.astype(BF16)
        m_scr[h] = m
    for h in heads:
        acc_scr[h] = _dot(vt_ref[0, h, i], p_scr[h])

    def attend(blocks):
        for u, j in enumerate(blocks):
            for h in heads:
                s_scr[u * N_HEADS + h] = _dot(kh_ref[0, h, j], qa_scr[h])
        m_old = [m_scr[h] for h in heads]
        m_new, alpha = [], []
        for h in heads:
            ss = [s_scr[u * N_HEADS + h] for u in range(len(blocks))]
            top = m_old[h]
            for s in ss:
                top = jnp.maximum(top, jnp.max(s, axis=0, keepdims=True))
            m_new.append(top)
            alpha.append(jnp.exp2(m_old[h] - top))
            for u, s in enumerate(ss):
                p_scr[u * N_HEADS + h] = jnp.exp2(s - top).astype(BF16)
        for h in heads:
            m_scr[h] = m_new[h]
        acc_old = [acc_scr[h] for h in heads]
        pv = []
        for h in heads:
            t = _dot(vt_ref[0, h, blocks[0]], p_scr[h])
            for u in range(1, len(blocks)):
                t = t + _dot(vt_ref[0, h, blocks[u]], p_scr[u * N_HEADS + h])
            pv.append(t)
        for h in heads:
            acc_scr[h] = alpha[h] * acc_old[h] + pv[h]

    def pair(t, carry):
        attend([BLOCKS_PER_PASS * t + u for u in range(BLOCKS_PER_PASS)])
        return carry

    lax.fori_loop(0, i // BLOCKS_PER_PASS, pair, 0)

    def single(j, carry):
        attend([j])
        return carry

    lax.fori_loop(i // BLOCKS_PER_PASS * BLOCKS_PER_PASS, i, single, 0)
    for h in heads:
        acc = acc_scr[h]
        o_ref[0, h] = acc[:HEAD_DIM] / acc[HEAD_DIM:HEAD_DIM + 1]


def _moba_prompt(slopes, qt, kh, vt, kmh):
    nb, _, nblk = kh.shape[:3]
    resident = lambda shape: pl.BlockSpec(shape, lambda b, i, s: (b, 0, 0, 0, 0), pipeline_mode=pl.Buffered(1))
    return pl.pallas_call(
        _moba_prompt_kernel,
        grid_spec=pltpu.PrefetchScalarGridSpec(
            num_scalar_prefetch=1,
            grid=(nb, nblk),
            in_specs=[pl.BlockSpec((1, N_HEADS, HEAD_DIM, MOBA_BLOCK), lambda b, i, s: (b, 0, 0, i)),
                      resident((1, N_HEADS, nblk, MOBA_BLOCK, KEY_WIDTH)),
                      resident((1, N_HEADS, nblk, VAL_ROWS, MOBA_BLOCK)),
                      pl.BlockSpec((1, N_HEADS, MAX_BLOCKS, HEAD_DIM), lambda b, i, s: (b, 0, 0, 0))],
            out_specs=pl.BlockSpec((1, N_HEADS, HEAD_DIM, MOBA_BLOCK), lambda b, i, s: (b, 0, 0, i)),
            scratch_shapes=[pltpu.VMEM((N_HEADS, KEY_WIDTH, MOBA_BLOCK), BF16),
                            pltpu.VMEM((BLOCKS_PER_PASS * N_HEADS, MOBA_BLOCK, MOBA_BLOCK), F32),
                            pltpu.VMEM((BLOCKS_PER_PASS * N_HEADS, MOBA_BLOCK, MOBA_BLOCK), BF16),
                            pltpu.VMEM((N_HEADS, 1, MOBA_BLOCK), F32),
                            pltpu.VMEM((N_HEADS, VAL_ROWS, MOBA_BLOCK), F32)]),
        out_shape=jax.ShapeDtypeStruct((nb, N_HEADS, HEAD_DIM, nblk * MOBA_BLOCK), F32),
        compiler_params=_cparams(("parallel", "arbitrary")),
        name="moba_prompt",
    )(slopes, qt, kh, vt, kmh)


def _hgrn_gates(zf, lb):
    log_sig = jnp.minimum(zf, 0.0) - jnp.log(1.0 + jnp.exp(-jnp.abs(zf)))
    a = jnp.log(lb)
    b = jnp.log(1.0 - lb) + log_sig
    logf = jnp.maximum(a, b) + jnp.log(1.0 + jnp.exp(-jnp.abs(a - b)))
    k = (1.0 - lb) / (1.0 + jnp.exp(zf))
    return logf, k


def _hgrn_prompt_kernel(zq_ref, zf_ref, zi_ref, zg_ref, lb_ref, go_ref, tri_ref,
                        o_ref, s_ref, st_scr, oi_scr):
    i = pl.program_id(1)
    tb = zq_ref.shape[0]

    @pl.when(i == 0)
    def _():
        st_scr[...] = jnp.zeros_like(st_scr)

    tri = tri_ref[...]
    n_chunks = tb // HG_CHUNK
    head_cols = [slice(h * HG_HEAD_DIM, (h + 1) * HG_HEAD_DIM) for h in range(HG_HEADS)]

    heads = []
    for cols in head_cols:
        v = zi_ref[:, cols]
        logf, k = _hgrn_gates(zf_ref[:, cols], lb_ref[:, cols])
        b = _dot_f32_rhs(tri, logf)
        b3 = b.reshape(n_chunks, HG_CHUNK, HG_HEAD_DIM)
        total = jnp.broadcast_to(b3[:, HG_CHUNK - 1:, :], b3.shape).reshape(tb, HG_HEAD_DIM)
        rest = total - b
        q_in = (zq_ref[:, cols] * jnp.exp(b)).astype(BF16)
        k_in = (k * jnp.exp(-b)).astype(BF16)
        k_out = (k * jnp.exp(rest)).astype(BF16)
        dec = jnp.exp(total)
        vb = v.astype(BF16)
        att = jnp.where(tri > 0, _dot_nt(q_in, k_in), 0.0).astype(BF16)
        o_intra = _dot(att, vb)
        heads.append((q_in, k_out, dec, vb, o_intra))

    sts = [st_scr[h] for h in range(HG_HEADS)]
    for n in range(tb // HG_CHUNK):
        rows = slice(n * HG_CHUNK, (n + 1) * HG_CHUNK)
        for h, cols in enumerate(head_cols):
            q_in, k_out, dec, vb, _ = heads[h]
            oi_scr[rows, cols] = _dot_nt(q_in[rows], sts[h].astype(BF16))
            update = lax.dot_general(vb[rows], k_out[rows], (((0,), (0,)), ((), ())),
                                     preferred_element_type=F32)
            sts[h] = sts[h] * dec[n * HG_CHUNK:n * HG_CHUNK + 1, :] + update

    go = go_ref[...]
    for h, cols in enumerate(head_cols):
        st_scr[h] = sts[h]
        o = heads[h][4] + oi_scr[:, cols]
        o_ref[:, cols] = _rms(o, go) * _silu(zg_ref[:, cols])

    @pl.when(i == pl.num_programs(1) - 1)
    def _():
        for h in range(HG_HEADS):
            s_ref[0, h] = sts[h].T


def _hgrn_prompt(z, lb, g_out, tri, nb, seq):
    tb = MOBA_BLOCK
    nt = seq // tb
    zspec = lambda c: pl.BlockSpec((tb, HG_WIDTH), lambda b, i, c=c: (b * nt + i, c))
    sq = pl.BlockSpec((tb, tb), lambda b, i: (0, 0))
    return pl.pallas_call(
        _hgrn_prompt_kernel,
        grid=(nb, nt),
        in_specs=[zspec(COL_ZQ), zspec(COL_ZF), zspec(COL_ZI), zspec(COL_ZG),
                  pl.BlockSpec((1, HG_WIDTH), lambda b, i: (0, 0)),
                  pl.BlockSpec((1, HG_HEAD_DIM), lambda b, i: (0, 0)),
                  sq],
        out_specs=[pl.BlockSpec((tb, HG_WIDTH), lambda b, i: (b * nt + i, 0)),
                   pl.BlockSpec((1, HG_HEADS, HG_HEAD_DIM, HG_HEAD_DIM), lambda b, i: (b, 0, 0, 0))],
        out_shape=[jax.ShapeDtypeStruct((nb * seq, HG_WIDTH), F32),
                   jax.ShapeDtypeStruct((nb, HG_HEADS, HG_HEAD_DIM, HG_HEAD_DIM), F32)],
        scratch_shapes=[pltpu.VMEM((HG_HEADS, HG_HEAD_DIM, HG_HEAD_DIM), F32),
                        pltpu.VMEM((tb, HG_WIDTH), F32)],
        compiler_params=_cparams(("parallel", "arbitrary")),
        name="hgrn_prompt",
    )(z, z, z, z, lb, g_out, tri)


def _hgrn_sample_kernel(zq_ref, zf_ref, zi_ref, zg_ref, lb_ref, go_ref, s_ref, o_ref, sn_ref):
    db = zq_ref.shape[0]
    eye = (lax.broadcasted_iota(jnp.int32, (HG_HEAD_DIM, HG_HEAD_DIM), 0)
           == lax.broadcasted_iota(jnp.int32, (HG_HEAD_DIM, HG_HEAD_DIM), 1))
    col = lambda r: jnp.sum(jnp.where(eye, r, 0.0), axis=1, keepdims=True)
    lb = lb_ref[...]
    go = go_ref[...]

    def body(b, carry):
        row = pl.ds(b, 1)
        logf, k = _hgrn_gates(zf_ref[row, :], lb)
        s_new = s_ref[0, b, 0] * col(jnp.exp(logf)) + col(k) * zi_ref[row, :]
        sn_ref[b, 0] = s_new
        o = jnp.sum(s_new * col(zq_ref[row, :]), axis=0, keepdims=True)
        o_ref[row, :] = _rms(o, go) * _silu(zg_ref[row, :])
        return carry

    lax.fori_loop(0, db, body, 0)


def _hgrn_sample(z, lb, g_out, state, layer, row_blk, db):
    zspec = lambda c: pl.BlockSpec((db, HG_HEAD_DIM), lambda h, c=c: (row_blk, 4 * c + h))
    return pl.pallas_call(
        _hgrn_sample_kernel,
        grid=(HG_HEADS,),
        in_specs=[zspec(COL_ZQ), zspec(COL_ZF), zspec(COL_ZI), zspec(COL_ZG),
                  pl.BlockSpec((1, HG_HEAD_DIM), lambda h: (0, h)),
                  pl.BlockSpec((1, HG_HEAD_DIM), lambda h: (0, 0)),
                  pl.BlockSpec((1, db, 1, HG_HEAD_DIM, HG_HEAD_DIM), lambda h: (layer, 0, h, 0, 0))],
        out_specs=[pl.BlockSpec((db, HG_HEAD_DIM), lambda h: (0, h)),
                   pl.BlockSpec((db, 1, HG_HEAD_DIM, HG_HEAD_DIM), lambda h: (0, h, 0, 0))],
        out_shape=[jax.ShapeDtypeStruct((db, HG_WIDTH), F32),
                   jax.ShapeDtypeStruct((db, HG_HEADS, HG_HEAD_DIM, HG_HEAD_DIM), F32)],
        compiler_params=_cparams(("parallel",)),
        name="hgrn_sample",
    )(z, z, z, z, lb, g_out, state)


def _cache_mean_kernel(pt_ref, *refs):
    pages, o_ref = refs[:-1], refs[-1]
    g = pl.program_id(1)
    per_step = len(pages) // PAGES_PER_BLOCK
    ones = jnp.ones((8, PAGE_SIZE), BF16)
    for c in range(per_step):
        tot = pages[PAGES_PER_BLOCK * c][0, 0]
        for p in range(1, PAGES_PER_BLOCK):
            tot = tot + pages[PAGES_PER_BLOCK * c + p][0, 0]
        row = sum(_dot_nt(ones, t) for t in _split3(tot))[0:1]
        o_ref[0, pl.ds(g * per_step + c, 1), :] = row * (1.0 / MOBA_BLOCK)


def _cache_mean(page_table, cache_t, layer, db, n_blk):
    n_pages = n_blk * PAGES_PER_BLOCK
    pg = PAGE_GROUP if n_pages % PAGE_GROUP == 0 else PAGES_PER_BLOCK
    spec = lambda k: pl.BlockSpec((1, 1, ATT_WIDTH, PAGE_SIZE),
                                  lambda b, g, pt, k=k: (layer, pt[b, g * pg + k], 0, 0))
    return pl.pallas_call(
        _cache_mean_kernel,
        grid_spec=pltpu.PrefetchScalarGridSpec(
            num_scalar_prefetch=1,
            grid=(db, n_pages // pg),
            in_specs=[spec(k) for k in range(pg)],
            out_specs=pl.BlockSpec((1, n_blk, ATT_WIDTH), lambda b, g, pt: (b, 0, 0))),
        out_shape=jax.ShapeDtypeStruct((db, n_blk, ATT_WIDTH), F32),
        compiler_params=_cparams(("parallel", "arbitrary")),
        name="cache_block_mean",
    )(page_table, *([cache_t] * pg))


def _sample_select_kernel(q_ref, km_ref, hm_ref, o_ref):
    b = pl.program_id(0)
    km = km_ref[0].astype(BF16).astype(F32)
    q = q_ref[pl.ds(b, 1), :].astype(BF16).astype(F32)
    gate = _dot_f32_lhs(km * q, hm_ref[...])
    _, picks = _top_mask_axis0(gate, MOBA_TOPK)
    rows = picks + [jnp.zeros_like(picks[0])] * (o_ref.shape[1] - len(picks))
    o_ref[0] = jnp.concatenate(rows, axis=0)


def _sample_select(q_s, km_s, head_mat):
    db, n_blk, _ = km_s.shape
    return pl.pallas_call(
        _sample_select_kernel,
        grid=(db,),
        in_specs=[pl.BlockSpec((db, ATT_WIDTH), lambda b: (0, 0)),
                  pl.BlockSpec((1, n_blk, ATT_WIDTH), lambda b: (b, 0, 0)),
                  pl.BlockSpec((ATT_WIDTH, LANES), lambda b: (0, 0))],
        out_specs=pl.BlockSpec((1, 8, LANES), lambda b: (b, 0, 0)),
        out_shape=jax.ShapeDtypeStruct((db, 8, LANES), jnp.int32),
        compiler_params=_cparams(("parallel",)),
        name="sample_select",
    )(q_s, km_s, head_mat)


def _moba_sample_kernel(past_len, phys_ref, blk_ref, slopes_ref, q_ref, k_ref, v_ref, *refs):
    pages_per_head = MOBA_TOPK * PAGES_PER_BLOCK
    n_pages = SAMPLE_HEADS_PER_STEP * pages_per_head
    k_pages, v_pages, o_ref = refs[:n_pages], refs[n_pages:2 * n_pages], refs[-1]
    b = pl.program_id(0)
    m_keys = MOBA_TOPK * MOBA_BLOCK
    lane = lax.broadcasted_iota(jnp.int32, (1, m_keys), 1)
    for u in range(SAMPLE_HEADS_PER_STEP):
        h = pl.program_id(1) * SAMPLE_HEADS_PER_STEP + u
        head = pl.ds(h, 1)
        mine = slice(u * pages_per_head, (u + 1) * pages_per_head)
        q = q_ref[0, head, :] * (HEAD_DIM ** -0.5)
        kt = jnp.concatenate([r[0, 0, 0] for r in k_pages[mine]], axis=1).astype(BF16)
        vt = jnp.concatenate([r[0, 0, 0] for r in v_pages[mine]], axis=1).astype(BF16)
        s = _dot(jnp.broadcast_to(q, (8, HEAD_DIM)).astype(BF16), kt)[0:1]
        kblk = blk_ref[b, h * MOBA_TOPK + MOBA_TOPK - 1]
        for t in range(MOBA_TOPK - 2, -1, -1):
            kblk = jnp.where(lane < (t + 1) * MOBA_BLOCK, blk_ref[b, h * MOBA_TOPK + t], kblk)
        kpos = kblk * MOBA_BLOCK + lane % MOBA_BLOCK
        s = s - slopes_ref[h] * (past_len - kpos).astype(F32)
        s_self = jnp.sum(q * k_ref[0, head, :], axis=1, keepdims=True)
        m = jnp.maximum(jnp.max(s, axis=1, keepdims=True), s_self)
        p = jnp.exp(s - m)
        p_self = jnp.exp(s_self - m)
        l = jnp.sum(p, axis=1, keepdims=True) + p_self
        o = _dot_nt(jnp.broadcast_to(p, (8, m_keys)).astype(BF16), vt)[0:1] + p_self * v_ref[0, head, :]
        o_ref[0, u] = o / l


def _moba_sample(phys, sel_blk, slopes, q_s, k_s, v_s, cache_kt, cache_vt, layer, past_len):
    db = q_s.shape[0]
    pages_per_head = MOBA_TOPK * PAGES_PER_BLOCK
    hps = SAMPLE_HEADS_PER_STEP
    n_pages = hps * pages_per_head
    row = pl.BlockSpec((1, N_HEADS, HEAD_DIM), lambda b, g, ph, sb, sl: (b, 0, 0))
    page = lambda t: pl.BlockSpec(
        (1, 1, 1, HEAD_DIM, PAGE_SIZE),
        lambda b, g, ph, sb, sl, t=t: (layer, ph[b, g * n_pages + t], g * hps + t // pages_per_head, 0, 0))
    pages = [page(t) for t in range(n_pages)]
    r3 = lambda a: a.reshape(db, N_HEADS, HEAD_DIM)
    out = pl.pallas_call(
        functools.partial(_moba_sample_kernel, past_len),
        grid_spec=pltpu.PrefetchScalarGridSpec(
            num_scalar_prefetch=3,
            grid=(db, N_HEADS // hps),
            in_specs=[row, row, row] + pages + pages,
            out_specs=pl.BlockSpec((1, hps, 1, HEAD_DIM), lambda b, g, ph, sb, sl: (b, g, 0, 0))),
        out_shape=jax.ShapeDtypeStruct((db, N_HEADS, 1, HEAD_DIM), F32),
        compiler_params=_cparams(("parallel", "arbitrary")),
        name="moba_sample",
    )(phys, sel_blk, slopes, r3(q_s), r3(k_s), r3(v_s), *([cache_kt] * n_pages), *([cache_vt] * n_pages))
    return out.reshape(db, ATT_WIDTH)


def _merge_kernel(ob_transposed, x_ref, oa_ref, ob_ref, ga0, ga1, gb0, gb1, wa_ref, wb_ref, wo_ref, o_ref):
    oa = oa_ref[...].astype(BF16)
    ob = ob_ref[0].T if ob_transposed else ob_ref[...]
    a = _dot(oa, wa_ref[...])
    bm = _dot(ob.astype(BF16), wb_ref[...])
    half = D_MODEL // 2
    m0 = _sigmoid(ga0[...]) * a[:, :half] + _sigmoid(gb0[...]) * bm[:, :half]
    m1 = _sigmoid(ga1[...]) * a[:, half:] + _sigmoid(gb1[...]) * bm[:, half:]
    y = _dot(m0.astype(BF16), wo_ref[:half, :]) + _dot(m1.astype(BF16), wo_ref[half:, :])
    o_ref[...] = x_ref[...] + y


def _merge(x, z, o_a, o_b, wa, wb, wo, *, tile, row_blk0, n_tiles, ob_transposed, tiles_per_seq=None):
    rspec = lambda w, c: pl.BlockSpec((tile, w), lambda i, c=c: (row_blk0 + i, c))
    if ob_transposed:
        ob_spec = pl.BlockSpec((1, ATT_WIDTH, tile), lambda i: (i // tiles_per_seq, 0, i % tiles_per_seq))
    else:
        ob_spec = pl.BlockSpec((tile, ATT_WIDTH), lambda i: (i, 0))
    wspec = lambda r: pl.BlockSpec((r, D_MODEL), lambda i: (0, 0))
    return pl.pallas_call(
        functools.partial(_merge_kernel, ob_transposed),
        grid=(n_tiles,),
        in_specs=[rspec(D_MODEL, 0),
                  pl.BlockSpec((tile, HG_WIDTH), lambda i: (i, 0)), ob_spec,
                  rspec(512, COL_GA), rspec(512, COL_GA + 1), rspec(512, COL_GB), rspec(512, COL_GB + 1),
                  wspec(HG_WIDTH), wspec(ATT_WIDTH), wspec(D_MODEL)],
        out_specs=rspec(D_MODEL, 0),
        out_shape=jax.ShapeDtypeStruct(x.shape, F32),
        input_output_aliases={0: 0},
        compiler_params=_cparams(("parallel",)),
        name="merge_t" if ob_transposed else "merge",
    )(x, o_a, o_b, z, z, z, z, wa, wb, wo)


def _ffn_kernel(x_ref, g_ref, wg_ref, wu_ref, wd_ref, o_ref, h_scr, acc_scr):
    j = pl.program_id(1)

    @pl.when(j == 0)
    def _():
        h_scr[...] = _rms(x_ref[...], g_ref[...]).astype(BF16)
        acc_scr[...] = jnp.zeros_like(acc_scr)

    h = h_scr[...]
    a = _silu(_dot(h, wg_ref[...])) * _dot(h, wu_ref[...])
    acc_scr[...] += _dot(a.astype(BF16), wd_ref[...])

    @pl.when(j == pl.num_programs(1) - 1)
    def _():
        o_ref[...] = x_ref[...] + acc_scr[...]


def _ffn_dense(x, g, wg, wu, wd):
    n = x.shape[0]
    tm = _row_tile(n, 512)
    return pl.pallas_call(
        _ffn_kernel,
        grid=(pl.cdiv(n, tm), D_FF // FF_TILE),
        in_specs=[pl.BlockSpec((tm, D_MODEL), lambda i, j: (i, 0)),
                  pl.BlockSpec((1, D_MODEL), lambda i, j: (0, 0)),
                  pl.BlockSpec((D_MODEL, FF_TILE), lambda i, j: (0, j)),
                  pl.BlockSpec((D_MODEL, FF_TILE), lambda i, j: (0, j)),
                  pl.BlockSpec((FF_TILE, D_MODEL), lambda i, j: (j, 0))],
        out_specs=pl.BlockSpec((tm, D_MODEL), lambda i, j: (i, 0)),
        out_shape=jax.ShapeDtypeStruct((n, D_MODEL), F32),
        scratch_shapes=[pltpu.VMEM((tm, D_MODEL), BF16), pltpu.VMEM((tm, D_MODEL), F32)],
        compiler_params=_cparams(("parallel", "arbitrary")),
        name="ffn_dense",
    )(x, g.reshape(1, D_MODEL), wg, wu, wd)


def _router_kernel(x_ref, g_ref, wr_ref, h_ref, logit_ref):
    h = _rms(x_ref[...], g_ref[...])
    h_ref[...] = h
    logit_ref[...] = _dot(h.astype(BF16), wr_ref[...].astype(BF16))


def _router(x, g, wr_pad):
    n = x.shape[0]
    tm = _row_tile(n, 1024)
    return pl.pallas_call(
        _router_kernel,
        grid=(pl.cdiv(n, tm),),
        in_specs=[pl.BlockSpec((tm, D_MODEL), lambda i: (i, 0)),
                  pl.BlockSpec((1, D_MODEL), lambda i: (0, 0)),
                  pl.BlockSpec((D_MODEL, LANES), lambda i: (0, 0))],
        out_specs=[pl.BlockSpec((tm, D_MODEL), lambda i: (i, 0)),
                   pl.BlockSpec((tm, LANES), lambda i: (i, 0))],
        out_shape=[jax.ShapeDtypeStruct((n, D_MODEL), F32),
                   jax.ShapeDtypeStruct((n, LANES), F32)],
        compiler_params=_cparams(("parallel",)),
        name="moe_router",
    )(x, g.reshape(1, D_MODEL), wr_pad)


def _expert_kernel(be_ref, used_ref, x_ref, wg_ref, wu_ref, wd_ref, o_ref, acc_scr):
    i = pl.program_id(0)
    j = pl.program_id(1)

    @pl.when(j == 0)
    def _():
        acc_scr[...] = jnp.zeros_like(acc_scr)

    @pl.when(used_ref[i] > 0)
    def _():
        x = x_ref[...].astype(BF16)
        a = _silu(_dot(x, wg_ref[0])) * _dot(x, wu_ref[0])
        acc_scr[...] += _dot(a.astype(BF16), wd_ref[0])

    @pl.when(j == pl.num_programs(1) - 1)
    def _():
        o_ref[...] = acc_scr[...]


def _experts(blk_e, blk_used, xb, weg, weu, wed):
    rows = xb.shape[0] // MOE_ROWS * MOE_ROWS
    return pl.pallas_call(
        _expert_kernel,
        grid_spec=pltpu.PrefetchScalarGridSpec(
            num_scalar_prefetch=2,
            grid=(rows // MOE_ROWS, D_FF // FF_TILE),
            in_specs=[pl.BlockSpec((MOE_ROWS, D_MODEL), lambda i, j, be, bu: (i, 0)),
                      pl.BlockSpec((1, D_MODEL, FF_TILE), lambda i, j, be, bu: (be[i], 0, j)),
                      pl.BlockSpec((1, D_MODEL, FF_TILE), lambda i, j, be, bu: (be[i], 0, j)),
                      pl.BlockSpec((1, FF_TILE, D_MODEL), lambda i, j, be, bu: (be[i], j, 0))],
            out_specs=pl.BlockSpec((MOE_ROWS, D_MODEL), lambda i, j, be, bu: (i, 0)),
            scratch_shapes=[pltpu.VMEM((MOE_ROWS, D_MODEL), F32)]),
        out_shape=jax.ShapeDtypeStruct((rows, D_MODEL), F32),
        compiler_params=_cparams(("parallel", "arbitrary")),
        name="moe_experts",
    )(blk_e, blk_used, xb, weg, weu, wed)


def _row_copy(src, src_row, dst, dst_row, sem):
    return pltpu.make_async_copy(src.at[pl.ds(src_row, 1)], dst.at[pl.ds(dst_row, 1)], sem)


def _dispatch_kernel(dest_ref, h_ref, xb_in_ref, xb_ref, sem):
    del xb_in_ref
    tile = h_ref.shape[0]

    def start(r, carry):
        for k in range(TOP_K):
            _row_copy(h_ref, r, xb_ref, dest_ref[0, 0, r * TOP_K + k], sem).start()
        return carry

    def wait(r, carry):
        for k in range(TOP_K):
            _row_copy(h_ref, r, xb_ref, dest_ref[0, 0, r * TOP_K + k], sem).wait()
        return carry

    lax.fori_loop(0, tile, start, 0, unroll=DMA_UNROLL)
    lax.fori_loop(0, tile, wait, 0, unroll=DMA_UNROLL)


def _dispatch(dest, h, n_rows):
    n = h.shape[0]
    tile = _row_tile(n, 512)
    n_tiles = pl.cdiv(n, tile)
    dest_p = jnp.pad(dest, ((0, n_tiles * tile - n), (0, 0)), constant_values=n_rows)
    hbm = pl.BlockSpec(memory_space=pl.ANY)
    return pl.pallas_call(
        _dispatch_kernel,
        grid=(n_tiles,),
        in_specs=[pl.BlockSpec((1, 1, tile * TOP_K), lambda i: (i, 0, 0), memory_space=pltpu.SMEM),
                  pl.BlockSpec((tile, D_MODEL), lambda i: (i, 0)), hbm],
        out_specs=hbm,
        out_shape=jax.ShapeDtypeStruct((n_rows + 8, D_MODEL), F32),
        scratch_shapes=[pltpu.SemaphoreType.DMA(())],
        input_output_aliases={2: 0},
        compiler_params=_cparams(("arbitrary",)),
        name="moe_dispatch",
    )(dest_p.reshape(n_tiles, 1, tile * TOP_K), h, jnp.zeros((n_rows + 8, D_MODEL), F32))


def _combine_kernel(slot_ref, x_ref, gate_ref, yb_ref, o_ref, ybuf, sem):
    tile = x_ref.shape[0]

    def start(r, carry):
        for k in range(TOP_K):
            _row_copy(yb_ref, slot_ref[0, 0, r * TOP_K + k], ybuf.at[k], r, sem).start()
        return carry

    def wait(r, carry):
        for k in range(TOP_K):
            _row_copy(yb_ref, slot_ref[0, 0, r * TOP_K + k], ybuf.at[k], r, sem).wait()
        return carry

    lax.fori_loop(0, tile, start, 0, unroll=DMA_UNROLL)
    lax.fori_loop(0, tile, wait, 0, unroll=DMA_UNROLL)
    y = x_ref[...]
    for k in range(TOP_K):
        y = y + gate_ref[:, k:k + 1] * ybuf[k]
    o_ref[...] = y


def _combine(x, gates, dest, yb):
    n = x.shape[0]
    tile = _row_tile(n, 512)
    n_tiles = pl.cdiv(n, tile)
    slot_p = jnp.pad(dest, ((0, n_tiles * tile - n), (0, 0))).reshape(n_tiles, 1, tile * TOP_K)
    return pl.pallas_call(
        _combine_kernel,
        grid=(n_tiles,),
        in_specs=[pl.BlockSpec((1, 1, tile * TOP_K), lambda i: (i, 0, 0), memory_space=pltpu.SMEM),
                  pl.BlockSpec((tile, D_MODEL), lambda i: (i, 0)),
                  pl.BlockSpec((tile, TOP_K), lambda i: (i, 0)),
                  pl.BlockSpec(memory_space=pl.ANY)],
        out_specs=pl.BlockSpec((tile, D_MODEL), lambda i: (i, 0)),
        out_shape=jax.ShapeDtypeStruct((n, D_MODEL), F32),
        scratch_shapes=[pltpu.VMEM((TOP_K, tile, D_MODEL), F32), pltpu.SemaphoreType.DMA(())],
        compiler_params=_cparams(("arbitrary",)),
        name="moe_combine",
    )(slot_p, x, gates, yb)


def _moe(x, g, wr, weg, weu, wed):
    n = x.shape[0]
    wr_pad = jnp.zeros((D_MODEL, LANES), F32).at[:, :N_EXPERTS].set(wr)
    h, logits = _router(x, g, wr_pad)
    top_l, top_e = lax.top_k(logits[:, :N_EXPERTS], TOP_K)
    gates = jax.nn.softmax(top_l, axis=-1)
    n_slots = n * TOP_K
    n_blocks = -(-(n_slots + N_EXPERTS * (MOE_ROWS - 1)) // MOE_ROWS)
    onehot = (top_e.reshape(n_slots, 1) == jnp.arange(N_EXPERTS)).astype(jnp.int32)
    upto = jnp.cumsum(onehot, axis=0)
    counts = upto[-1]
    padded = (counts + MOE_ROWS - 1) // MOE_ROWS * MOE_ROWS
    end_pad = jnp.cumsum(padded)
    start_pad = end_pad - padded
    dest = jnp.sum(onehot * (start_pad + upto - 1), axis=1).astype(jnp.int32).reshape(n, TOP_K)
    blk_start = jnp.arange(n_blocks) * MOE_ROWS
    blk_e = jnp.minimum(jnp.searchsorted(end_pad, blk_start, side='right'), N_EXPERTS - 1).astype(jnp.int32)
    blk_used = (blk_start < end_pad[-1]).astype(jnp.int32)
    xb = _dispatch(dest, h, n_blocks * MOE_ROWS)
    yb = _experts(blk_e, blk_used, xb, weg, weu, wed)
    return _combine(x, gates, dest, yb)


def _chunk_mask(tb):
    t = np.arange(tb)
    same = (t[:, None] // HG_CHUNK) == (t[None, :] // HG_CHUNK)
    return jnp.asarray(same & (t[None, :] <= t[:, None]), BF16)


def kernel(x_prompt, x_sample, cache_k, cache_v, state_hgrn, page_table, w_in, hg_lower, hg_norm, q_norm, k_norm, w_branch_a, w_branch_b, w_out, attn_norm, ffn_norm, w_dense_gate, w_dense_up, w_dense_down, w_router, w_exp_gate, w_exp_up, w_exp_down):
    nb, seq, _ = x_prompt.shape
    db, dec_seq, _ = x_sample.shape
    depth, n_pool = cache_k.shape[:2]
    n_pages = page_table.shape[1]
    past_len = n_pages * PAGE_SIZE
    assert dec_seq == 1 and seq % MOBA_BLOCK == 0 and past_len % MOBA_BLOCK == 0
    n_rows_p = nb * seq
    assert n_rows_p % db == 0 and db % 8 == 0
    nblk = seq // MOBA_BLOCK
    n_blk_s = past_len // MOBA_BLOCK
    assert MOBA_TOPK <= nblk <= MAX_BLOCKS and n_blk_s >= MOBA_TOPK
    row_blk_s = n_rows_p // db

    slopes = 2.0 ** (-8.0 * jnp.arange(1, N_HEADS + 1, dtype=F32) / N_HEADS)
    lb_cum = jnp.cumsum(jax.nn.softmax(hg_lower.astype(F32), axis=0), axis=0)
    lower = lb_cum - lb_cum[:1]

    lane = np.arange(ATT_WIDTH)
    gmat = jnp.asarray((lane[:, None] // HEAD_DIM) == (lane[None, :] // HEAD_DIM), BF16)
    head_mat = jnp.asarray((lane[:, None] // HEAD_DIM) == np.arange(LANES)[None, :], BF16)
    tri = _chunk_mask(MOBA_BLOCK)
    cache_kt = cache_k.transpose(0, 1, 3, 4, 2)
    cache_vt = cache_v.transpose(0, 1, 3, 4, 2)
    cache_kt4 = cache_kt.reshape(depth, n_pool, ATT_WIDTH, PAGE_SIZE)

    x = jnp.concatenate([x_prompt.reshape(n_rows_p, D_MODEL), x_sample.reshape(db, D_MODEL)], axis=0)
    outs = {name: [] for name in ("kp", "vp", "sp", "ks", "vs", "ss")}
    for l in range(depth):
        z = _proj(x, attn_norm[l], w_in[l].astype(BF16))
        lb = lower[l].reshape(1, HG_WIDTH)
        g_out = hg_norm[l].reshape(1, HG_HEAD_DIM)
        qn = jnp.tile(q_norm[l], N_HEADS).reshape(1, ATT_WIDTH)
        kn = jnp.tile(k_norm[l], N_HEADS).reshape(1, ATT_WIDTH)
        wa, wb, wo = (w[l].astype(BF16) for w in (w_branch_a, w_branch_b, w_out))

        oa_p, s_p = _hgrn_prompt(z, lb, g_out, tri, nb, seq)
        k_p, v_p, qt, kh, vt, km = _qk_prompt(z, qn, kn, gmat, nb, nblk)
        kmh = km.reshape(nb, nblk, N_HEADS, HEAD_DIM).transpose(0, 2, 1, 3)
        kmh = jnp.pad(kmh, ((0, 0), (0, 0), (0, MAX_BLOCKS - nblk), (0, 0)))
        ob_p = _moba_prompt(slopes, qt, kh, vt, kmh).reshape(nb, ATT_WIDTH, seq)

        oa_s, s_s = _hgrn_sample(z, lb, g_out, state_hgrn, l, row_blk_s, db)
        q_s, k_s, v_s = _qk_sample(z, qn, kn, gmat, row_blk_s, db)
        km_s = _cache_mean(page_table, cache_kt4, l, db, n_blk_s)
        sel = _sample_select(q_s, km_s, head_mat)[:, :MOBA_TOPK, :N_HEADS]
        sel_blk = sel.transpose(0, 2, 1)
        lpage = sel_blk[..., None] * PAGES_PER_BLOCK + jnp.arange(PAGES_PER_BLOCK)
        phys = jnp.take_along_axis(page_table, lpage.reshape(db, -1), axis=1)
        ob_s = _moba_sample(phys, sel_blk.reshape(db, -1), slopes, q_s, k_s, v_s, cache_kt, cache_vt,
                            l, past_len)

        mt = 2 * MOBA_BLOCK if seq % (2 * MOBA_BLOCK) == 0 else MOBA_BLOCK
        x = _merge(x, z, oa_p, ob_p, wa, wb, wo, tile=mt, row_blk0=0, n_tiles=n_rows_p // mt,
                   ob_transposed=True, tiles_per_seq=seq // mt)
        x = _merge(x, z, oa_s, ob_s, wa, wb, wo, tile=db, row_blk0=row_blk_s, n_tiles=1,
                   ob_transposed=False)

        i = l // 2
        if l % 2 == 0:
            x = _ffn_dense(x, ffn_norm[l], w_dense_gate[i].astype(BF16), w_dense_up[i].astype(BF16),
                           w_dense_down[i].astype(BF16))
        else:
            x = _moe(x, ffn_norm[l], w_router[i], w_exp_gate[i].astype(BF16), w_exp_up[i].astype(BF16),
                     w_exp_down[i].astype(BF16))

        outs["kp"].append(k_p)
        outs["vp"].append(v_p)
        outs["sp"].append(s_p)
        outs["ks"].append(k_s.reshape(db, 1, N_HEADS, HEAD_DIM))
        outs["vs"].append(v_s.reshape(db, 1, N_HEADS, HEAD_DIM))
        outs["ss"].append(s_s)

    seq_major = lambda rows: jnp.stack(rows).transpose(0, 1, 4, 2, 3)
    return (x[:n_rows_p].reshape(nb, seq, D_MODEL), x[n_rows_p:].reshape(db, 1, D_MODEL),
            seq_major(outs["kp"]), seq_major(outs["vp"]), jnp.stack(outs["sp"]),
            jnp.stack(outs["ks"]), jnp.stack(outs["vs"]), jnp.stack(outs["ss"]))
```
